```python
import math
import jax, jax.numpy as jnp
from jax import lax
import numpy as np

D_MODEL = 2048
BATCH = 2
SEQ = 16384
DEPTH = 1
DEC_BATCH = 1
DEC_SEQ = 8192
PAST_LEN = 128

HEAD_DIM = 128
A_HEADS = 6
A_KV_HEADS = 2
A_GROUP = A_HEADS // A_KV_HEADS
WINDOW = 128
BLOCK = 128
N_BUCKETS = 32
MAX_DISTANCE = 128
B_HEADS = 6
Q_LORA = 512
KV_LORA = 512
QK_NOPE = 128
QK_ROPE = 64
V_DIM = 128
ROPE_THETA = 10000.0
Q_BLOCK = 128
C_HEADS = 4
N_MEM = 256
N_BRANCH = 3
D_FF = -(-8 * D_MODEL // (3 * 256)) * 256
ALPHA = (2 * DEPTH) ** 0.25
BETA = (8 * DEPTH) ** -0.25
LN_EPS = 1e-5
RMS_EPS = 1e-6
NEG = -1e30
IN_WIDTHS = (A_HEADS * HEAD_DIM, A_KV_HEADS * HEAD_DIM, A_KV_HEADS * HEAD_DIM, Q_LORA, KV_LORA, QK_ROPE, C_HEADS * HEAD_DIM)
D_IN = sum(IN_WIDTHS)

kernel_name = "hybrid_gated_window_mla_memxattn_encoder"


def layer_norm(x, g, b):
    xf = x.astype(jnp.float32)
    mu = jnp.mean(xf, axis=-1, keepdims=True)
    var = jnp.mean(jnp.square(xf - mu), axis=-1, keepdims=True)
    return ((xf - mu) * lax.rsqrt(var + LN_EPS) * g.astype(jnp.float32) + b.astype(jnp.float32)).astype(x.dtype)


def rms_norm(x, g):
    xf = x.astype(jnp.float32)
    ms = jnp.mean(jnp.square(xf), axis=-1, keepdims=True)
    return (xf * lax.rsqrt(ms + RMS_EPS) * g.astype(jnp.float32)).astype(x.dtype)


def rope_tables(S):
    half = QK_ROPE // 2
    inv = 1.0 / (ROPE_THETA ** (jnp.arange(half, dtype=jnp.float32) / half))
    ang = jnp.arange(S, dtype=jnp.float32)[:, None] * inv[None, :]
    return jnp.cos(ang), jnp.sin(ang)


def apply_rope(x, cos, sin):
    half = QK_ROPE // 2
    xf = x.astype(jnp.float32)
    x1, x2 = xf[..., :half], xf[..., half:]
    return jnp.concatenate([x1 * cos - x2 * sin, x2 * cos + x1 * sin], axis=-1).astype(x.dtype)


def t5_bucket(rel):
    half = N_BUCKETS // 2
    max_exact = half // 2
    ret = (rel > 0).astype(jnp.int32) * half
    n = jnp.abs(rel)
    large = max_exact + (jnp.log(jnp.maximum(n, 1).astype(jnp.float32) / max_exact)
                         / math.log(MAX_DISTANCE / max_exact) * (half - max_exact)).astype(jnp.int32)
    large = jnp.minimum(large, half - 1)
    return ret + jnp.where(n < max_exact, n, large)


def window_gqa(q, k, v, rel_bias, sink):
    B, S = q.shape[0], q.shape[1]
    nb = S // BLOCK
    qb = q.reshape(B, nb, BLOCK, A_KV_HEADS, A_GROUP, HEAD_DIM)

    def neighbours(t):
        t = t.reshape(B, nb, BLOCK, A_KV_HEADS, HEAD_DIM)
        tp = jnp.pad(t, ((0, 0), (1, 1), (0, 0), (0, 0), (0, 0)))
        return jnp.concatenate([tp[:, :-2], tp[:, 1:-1], tp[:, 2:]], axis=2)

    kw, vw = neighbours(k), neighbours(v)
    rel = (jnp.arange(3 * BLOCK) - BLOCK)[None, :] - jnp.arange(BLOCK)[:, None]
    band = jnp.abs(rel) <= WINDOW
    kblk = jnp.arange(nb)[:, None] + (jnp.arange(3 * BLOCK) // BLOCK)[None, :] - 1
    valid = (kblk >= 0) & (kblk < nb)
    mask = band[None] & valid[:, None, :]
    bias = rel_bias[t5_bucket(rel)].astype(jnp.float32)
    bias = bias.transpose(2, 0, 1).reshape(A_KV_HEADS, A_GROUP, BLOCK, 3 * BLOCK)
    s = jnp.einsum('bnqgrd,bnkgd->bngrqk', qb, kw).astype(jnp.float32) * (HEAD_DIM ** -0.5) + bias
    s = jnp.where(mask[None, :, None, None], s, NEG)
    sink_col = jnp.broadcast_to(sink.astype(jnp.float32).reshape(1, 1, A_KV_HEADS, A_GROUP, 1, 1), s.shape[:-1] + (1,))
    p = jax.nn.softmax(jnp.concatenate([s, sink_col], axis=-1), axis=-1)[..., :-1]
    o = jnp.einsum('bngrqk,bnkgd->bnqgrd', p.astype(v.dtype), vw)
    return o.reshape(B, S, A_HEADS * HEAD_DIM)


def mla_attend(q_nope, q_rope, k_nope, k_rope, v):
    B, S = q_nope.shape[0], q_nope.shape[1]
    nq = S // Q_BLOCK
    scale = (QK_NOPE + QK_ROPE) ** -0.5
    qn = jnp.moveaxis(q_nope.reshape(B, nq, Q_BLOCK, B_HEADS, QK_NOPE), 1, 0)
    qr = jnp.moveaxis(q_rope.reshape(B, nq, Q_BLOCK, B_HEADS, QK_ROPE), 1, 0)

    def block(args):
        qn_b, qr_b = args
        s = (jnp.einsum('bqhd,bkhd->bhqk', qn_b, k_nope) + jnp.einsum('bqhd,bkd->bhqk', qr_b, k_rope)).astype(jnp.float32) * scale
        p = jax.nn.softmax(s, axis=-1)
        return jnp.einsum('bhqk,bkhd->bqhd', p.astype(v.dtype), v)

    o = lax.map(block, (qn, qr))
    return jnp.moveaxis(o, 0, 1).reshape(B, S, B_HEADS * V_DIM)


def cross_attend(q, k, v):
    B, S = q.shape[0], q.shape[1]
    s = jnp.einsum('bqhd,bkhd->bhqk', q, k).astype(jnp.float32) * (HEAD_DIM ** -0.5)
    p = jax.nn.softmax(s, axis=-1)
    return jnp.einsum('bhqk,bkhd->bqhd', p.astype(v.dtype), v).reshape(B, S, C_HEADS * HEAD_DIM)


def encoder_layer(x, mem, w_in, rel_bias, sink, q_norm_g, w_uq, kv_norm_g, w_ukv, w_mem_kv,
                  w_gate, b_gate, w_br_a, w_br_b, w_br_c, w_o, ln1_g, ln1_b,
                  w_ffn_in, w_ffn_down, ln2_g, ln2_b):
    B, S, D = x.shape
    splits = [int(c) for c in np.cumsum(IN_WIDTHS)[:-1]]
    qa, ka, va, cq, ckv, kr, qc = jnp.split(x @ w_in, splits, axis=-1)

    a_out = window_gqa(qa.reshape(B, S, A_HEADS, HEAD_DIM), ka.reshape(B, S, A_KV_HEADS, HEAD_DIM),
                       va.reshape(B, S, A_KV_HEADS, HEAD_DIM), rel_bias, sink)

    cos, sin = rope_tables(S)
    qb = (rms_norm(cq, q_norm_g) @ w_uq).reshape(B, S, B_HEADS, QK_NOPE + QK_ROPE)
    q_nope = qb[..., :QK_NOPE]
    q_rope = apply_rope(qb[..., QK_NOPE:], cos[:, None, :], sin[:, None, :])
    kvb = (rms_norm(ckv, kv_norm_g) @ w_ukv).reshape(B, S, B_HEADS, QK_NOPE + V_DIM)
    k_nope, v_b = kvb[..., :QK_NOPE], kvb[..., QK_NOPE:]
    k_rope = apply_rope(kr, cos, sin)
    b_out = mla_attend(q_nope, q_rope, k_nope, k_rope, v_b)

    mkv = (mem @ w_mem_kv).reshape(B, N_MEM, 2, C_HEADS, HEAD_DIM)
    c_out = cross_attend(qc.reshape(B, S, C_HEADS, HEAD_DIM), mkv[:, :, 0], mkv[:, :, 1])

    g = jax.nn.sigmoid((x @ w_gate + b_gate).astype(jnp.float32)).astype(x.dtype).reshape(B, S, N_BRANCH, D)
    merged = g[:, :, 0] * (a_out @ w_br_a) + g[:, :, 1] * (b_out @ w_br_b) + g[:, :, 2] * (c_out @ w_br_c)
    h = layer_norm(ALPHA * x + merged @ w_o, ln1_g, ln1_b)

    gate, up = jnp.split(h @ w_ffn_in, 2, axis=-1)
    f = (jax.nn.silu(gate) * up) @ w_ffn_down
    return layer_norm(ALPHA * h + f, ln2_g, ln2_b)


def setup_inputs(seed: int = 0) -> dict:
    key = jax.random.key(seed)
    ks = iter(jax.random.split(key, 40))
    f32 = jnp.float32

    def nrm(shape, scale):
        return jax.random.normal(next(ks), shape, f32) * scale

    L, D = DEPTH, D_MODEL
    sd = D ** -0.5
    w_in_parts = []
    for i, w in enumerate(IN_WIDTHS):
        s = sd * BETA if i == 2 else sd
        w_in_parts.append(nrm((L, D, w), s))
    w_in = jnp.concatenate(w_in_parts, axis=-1)
    w_ukv = jnp.concatenate([nrm((L, KV_LORA, B_HEADS, 1, QK_NOPE), KV_LORA ** -0.5),
                             nrm((L, KV_LORA, B_HEADS, 1, V_DIM), KV_LORA ** -0.5 * BETA)], axis=3
                            ).reshape(L, KV_LORA, B_HEADS * (QK_NOPE + V_DIM))
    w_mem_kv = jnp.concatenate([nrm((L, D, C_HEADS * HEAD_DIM), sd),
                                nrm((L, D, C_HEADS * HEAD_DIM), sd * BETA)], axis=-1)
    return {
        "x_prompt": nrm((BATCH, SEQ, D), 1.0),
        "x_sample": nrm((DEC_BATCH, DEC_SEQ, D), 1.0),
        "mem_prompt": nrm((BATCH, N_MEM, D), 1.0),
        "mem_sample": nrm((DEC_BATCH, N_MEM, D), 1.0),
        "w_in": w_in,
        "rel_bias": nrm((N_BUCKETS, A_HEADS), 0.1),
        "sink": nrm((L, A_HEADS), 0.5),
        "q_norm_g": 1.0 + nrm((L, Q_LORA), 0.01),
        "w_uq": nrm((L, Q_LORA, B_HEADS * (QK_NOPE + QK_ROPE)), Q_LORA ** -0.5),
        "kv_norm_g": 1.0 + nrm((L, KV_LORA), 0.01),
        "w_ukv": w_ukv,
        "w_mem_kv": w_mem_kv,
        "w_gate": nrm((L, D, N_BRANCH * D), sd),
        "b_gate": nrm((L, N_BRANCH * D), 0.01),
        "w_br_a": nrm((L, A_HEADS * HEAD_DIM, D), (A_HEADS * HEAD_DIM) ** -0.5),
        "w_br_b": nrm((L, B_HEADS * V_DIM, D), (B_HEADS * V_DIM) ** -0.5),
        "w_br_c": nrm((L, C_HEADS * HEAD_DIM, D), (C_HEADS * HEAD_DIM) ** -0.5),
        "w_o": nrm((L, D, D), sd * BETA),
        "ln1_g": 1.0 + nrm((L, D), 0.01),
        "ln1_b": nrm((L, D), 0.01),
        "w_ffn_in": nrm((L, D, 2 * D_FF), sd * BETA),
        "w_ffn_down": nrm((L, D_FF, D), D_FF ** -0.5 * BETA),
        "ln2_g": 1.0 + nrm((L, D), 0.01),
        "ln2_b": nrm((L, D), 0.01),
    }


def reference(x_prompt, x_sample, mem_prompt, mem_sample, w_in, rel_bias, sink, q_norm_g, w_uq,
              kv_norm_g, w_ukv, w_mem_kv, w_gate, b_gate, w_br_a, w_br_b, w_br_c, w_o,
              ln1_g, ln1_b, w_ffn_in, w_ffn_down, ln2_g, ln2_b):
    def run(x, mem):
        for l in range(DEPTH):
            x = encoder_layer(x, mem, w_in[l], rel_bias, sink[l], q_norm_g[l], w_uq[l], kv_norm_g[l],
                              w_ukv[l], w_mem_kv[l], w_gate[l], b_gate[l], w_br_a[l], w_br_b[l],
                              w_br_c[l], w_o[l], ln1_g[l], ln1_b[l], w_ffn_in[l], w_ffn_down[l],
                              ln2_g[l], ln2_b[l])
        return x

    y_prompt = run(x_prompt, mem_prompt)
    y_sample = run(x_sample, mem_sample)
    return (y_prompt, y_sample)
```

```python
import functools
import math

import jax
import jax.numpy as jnp
import numpy as np
from jax import lax
from jax.experimental import pallas as pl
from jax.experimental.pallas import tpu as pltpu

F32 = jnp.float32
BF16 = jnp.bfloat16

D_MODEL = 2048
HEAD_DIM = 128
A_HEADS = 6
A_KV_HEADS = 2
A_GROUP = A_HEADS // A_KV_HEADS
WINDOW = 128
BLOCK = 128
N_BUCKETS = 32
MAX_DISTANCE = 128
B_HEADS = 6
Q_LORA = 512
KV_LORA = 512
QK_NOPE = 128
QK_ROPE = 64
V_DIM = 128
ROPE_THETA = 10000.0
C_HEADS = 4
N_MEM = 256
N_BRANCH = 3
D_FF = -(-8 * D_MODEL // (3 * 256)) * 256
DEPTH = 1
ALPHA = (2 * DEPTH) ** 0.25
LN_EPS = 1e-5
RMS_EPS = 1e-6
NEG = -1e30

QK_PAD = 256
ROPE_HALF = QK_ROPE // 2
VMEM_LIMIT = 56 * 1024 * 1024

_OFF_QA = 0
_OFF_KA = _OFF_QA + A_HEADS * HEAD_DIM
_OFF_VA = _OFF_KA + A_KV_HEADS * HEAD_DIM
_OFF_CQ = _OFF_VA + A_KV_HEADS * HEAD_DIM
_OFF_CKV = _OFF_CQ + Q_LORA
_OFF_QC = _OFF_CKV + KV_LORA
_OFF_KR = _OFF_QC + C_HEADS * HEAD_DIM
D_IN_PAD = _OFF_KR + 128


def _params(n_axes):
    return pltpu.CompilerParams(dimension_semantics=("arbitrary",) * n_axes,
                                vmem_limit_bytes=VMEM_LIMIT)


def _resident(shape, index_map):
    return pl.BlockSpec(shape, index_map, pipeline_mode=pl.Buffered(1))


def _dot(a, b):
    return jnp.dot(a, b, preferred_element_type=F32)


def _dot_nt(a, b):
    return lax.dot_general(a, b, (((1,), (1,)), ((), ())), preferred_element_type=F32)


def _layer_norm(v, g, b):
    mu = jnp.mean(v, axis=-1, keepdims=True)
    c = v - mu
    var = jnp.mean(c * c, axis=-1, keepdims=True)
    return c * lax.rsqrt(var + LN_EPS) * g + b


def _rms_norm(v, g):
    ms = jnp.mean(v * v, axis=-1, keepdims=True)
    return v * lax.rsqrt(ms + RMS_EPS) * g


def _rope_tile(t, c, sa, sb):
    return t * c + pltpu.roll(t, 128 - ROPE_HALF, 1) * sa + pltpu.roll(t, ROPE_HALF, 1) * sb


def _proj_kernel(x_ref, w_in_ref, qg_ref, w_uq_ref, kvg_ref, w_ukv_ref, c_ref, sa_ref, sb_ref,
                 xb_ref, qa_ref, ka_ref, va_ref, qc_ref, qm_ref, km_ref, vm_ref):
    xb = x_ref[...].astype(BF16)
    xb_ref[...] = xb
    proj = _dot(xb, w_in_ref[...])
    qa_ref[...] = (proj[:, _OFF_QA:_OFF_KA] * (HEAD_DIM ** -0.5)).astype(BF16)
    ka_ref[...] = proj[:, _OFF_KA:_OFF_VA].astype(BF16)
    va_ref[...] = proj[:, _OFF_VA:_OFF_CQ].astype(BF16)
    qc_ref[...] = (proj[:, _OFF_QC:_OFF_KR] * (HEAD_DIM ** -0.5)).astype(BF16)

    c, sa, sb = c_ref[...], sa_ref[...], sb_ref[...]
    cqn = _rms_norm(proj[:, _OFF_CQ:_OFF_CKV], qg_ref[...]).astype(BF16)
    qb = _dot(cqn, w_uq_ref[...])
    ckvn = _rms_norm(proj[:, _OFF_CKV:_OFF_QC], kvg_ref[...]).astype(BF16)
    kvb = _dot(ckvn, w_ukv_ref[...])
    kr = _rope_tile(proj[:, _OFF_KR:D_IN_PAD], c, sa, sb).astype(BF16)
    scale = (QK_NOPE + QK_ROPE) ** -0.5
    for h in range(B_HEADS):
        lo = h * QK_PAD
        qm_ref[:, lo:lo + QK_NOPE] = (qb[:, lo:lo + QK_NOPE] * scale).astype(BF16)
        qm_ref[:, lo + QK_NOPE:lo + QK_PAD] = (
            _rope_tile(qb[:, lo + QK_NOPE:lo + QK_PAD], c, sa, sb) * scale).astype(BF16)
        km_ref[:, lo:lo + QK_NOPE] = kvb[:, h * QK_NOPE:(h + 1) * QK_NOPE].astype(BF16)
        km_ref[:, lo + QK_NOPE:lo + QK_PAD] = kr
    vm_ref[...] = kvb[:, B_HEADS * QK_NOPE:].astype(BF16)


def _project(x2, w_in, q_norm_g, w_uq, kv_norm_g, w_ukv, rope_c, rope_sa, rope_sb, seq, tm=256):
    t = x2.shape[0]
    n_pos = seq // tm
    row = lambda i: (i, 0)
    fixed = lambda i: (0, 0)
    pos = lambda i: (i % n_pos, 0)
    widths = (D_MODEL, A_HEADS * HEAD_DIM, A_KV_HEADS * HEAD_DIM, A_KV_HEADS * HEAD_DIM,
              C_HEADS * HEAD_DIM, B_HEADS * QK_PAD, B_HEADS * QK_PAD, B_HEADS * V_DIM)
    return pl.pallas_call(
        _proj_kernel,
        grid=(t // tm,),
        in_specs=[
            pl.BlockSpec((tm, D_MODEL), row),
            _resident((D_MODEL, D_IN_PAD), fixed),
            _resident((1, Q_LORA), fixed),
            _resident((Q_LORA, B_HEADS * QK_PAD), fixed),
            _resident((1, KV_LORA), fixed),
            _resident((KV_LORA, B_HEADS * (QK_NOPE + V_DIM)), fixed),
            pl.BlockSpec((tm, 128), pos),
            pl.BlockSpec((tm, 128), pos),
            pl.BlockSpec((tm, 128), pos),
        ],
        out_specs=[pl.BlockSpec((tm, w), row) for w in widths],
        out_shape=[jax.ShapeDtypeStruct((t, w), BF16) for w in widths],
        compiler_params=_params(1),
        name="proj",
    )(x2, w_in, q_norm_g, w_uq, kv_norm_g, w_ukv, rope_c, rope_sa, rope_sb)


def _bias_kernel(rb_ref, bucket_ref, band_ref, o_ref):
    bucket = bucket_ref[...]
    band = band_ref[...] != 0
    for h in range(A_HEADS):
        acc = jnp.zeros(bucket.shape, F32)
        for b in range(N_BUCKETS):
            acc = jnp.where(bucket == b, rb_ref[b, h], acc)
        o_ref[h] = jnp.where(band, acc, NEG)


def _t5_bucket(rel):
    half = N_BUCKETS // 2
    max_exact = half // 2
    ret = (rel > 0).astype(jnp.int32) * half
    n = jnp.abs(rel)
    large = max_exact + (jnp.log(jnp.maximum(n, 1).astype(jnp.float32) / max_exact)
                         / math.log(MAX_DISTANCE / max_exact) * (half - max_exact)).astype(jnp.int32)
    large = jnp.minimum(large, half - 1)
    return ret + jnp.where(n < max_exact, n, large)


def _window_bias(rel_bias):
    rel = (jnp.arange(3 * BLOCK) - BLOCK)[None, :] - jnp.arange(BLOCK)[:, None]
    bucket = _t5_bucket(rel).astype(jnp.int32)
    band = (jnp.abs(rel) <= WINDOW).astype(jnp.int32)
    return pl.pallas_call(
        _bias_kernel,
        in_specs=[pl.BlockSpec(memory_space=pltpu.SMEM),
                  pl.BlockSpec(memory_space=pltpu.VMEM),
                  pl.BlockSpec(memory_space=pltpu.VMEM)],
        out_specs=pl.BlockSpec(memory_space=pltpu.VMEM),
        out_shape=jax.ShapeDtypeStruct((A_HEADS, BLOCK, 3 * BLOCK), F32),
        name="win_bias",
    )(rel_bias, bucket, band)


def _win_kernel(sink_ref, q_ref, kp_ref, kc_ref, kn_ref, vp_ref, vc_ref, vn_ref, bias_ref, o_ref,
                kbuf, vbuf, *, tq, seq):
    i = pl.program_id(1)
    kbuf[0:BLOCK] = kp_ref[...]
    kbuf[BLOCK:BLOCK + tq] = kc_ref[...]
    kbuf[BLOCK + tq:] = kn_ref[...]
    vbuf[0:BLOCK] = vp_ref[...]
    vbuf[BLOCK:BLOCK + tq] = vc_ref[...]
    vbuf[BLOCK + tq:] = vn_ref[...]
    rows = A_GROUP * BLOCK
    col = lax.broadcasted_iota(jnp.int32, (rows, 3 * BLOCK), 1)
    row_head = lax.broadcasted_iota(jnp.int32, (rows, 1), 0) // BLOCK
    for g in range(A_KV_HEADS):
        sink = jnp.zeros((rows, 1), F32)
        for r in range(A_GROUP):
            sink = jnp.where(row_head == r, sink_ref[g * A_GROUP + r], sink)
        bias = bias_ref[g * A_GROUP:(g + 1) * A_GROUP].reshape(rows, 3 * BLOCK)
        for j in range(tq // BLOCK):
            kpos = col + (i * tq + (j - 1) * BLOCK)
            valid = (kpos >= 0) & (kpos < seq)
            kw = kbuf[j * BLOCK:(j + 3) * BLOCK, g * HEAD_DIM:(g + 1) * HEAD_DIM]
            vw = vbuf[j * BLOCK:(j + 3) * BLOCK, g * HEAD_DIM:(g + 1) * HEAD_DIM]
            q = jnp.concatenate(
                [q_ref[j * BLOCK:(j + 1) * BLOCK, (g * A_GROUP + r) * HEAD_DIM:(g * A_GROUP + r + 1) * HEAD_DIM]
                 for r in range(A_GROUP)], axis=0)
            s = jnp.where(valid, _dot_nt(q, kw) + bias, NEG)
            m = jnp.maximum(jnp.max(s, axis=-1, keepdims=True), sink)
            p = jnp.exp(s - m)
            den = jnp.sum(p, axis=-1, keepdims=True) + jnp.exp(sink - m)
            o = _dot(p.astype(BF16), vw) / den
            for r in range(A_GROUP):
                h = g * A_GROUP + r
                o_ref[j * BLOCK:(j + 1) * BLOCK, h * HEAD_DIM:(h + 1) * HEAD_DIM] = (
                    o[r * BLOCK:(r + 1) * BLOCK].astype(BF16))


def _window_attention(qa, ka, va, bias, sink, tq=512):
    b, seq, _ = qa.shape
    nb = seq // BLOCK
    per = tq // BLOCK
    kvw = A_KV_HEADS * HEAD_DIM
    prev = lambda bi, i: (bi, jnp.maximum(i * per - 1, 0), 0)
    cur = lambda bi, i: (bi, i, 0)
    nxt = lambda bi, i: (bi, jnp.minimum((i + 1) * per, nb - 1), 0)
    return pl.pallas_call(
        functools.partial(_win_kernel, tq=tq, seq=seq),
        grid=(b, seq // tq),
        in_specs=[
            pl.BlockSpec(memory_space=pltpu.SMEM),
            pl.BlockSpec((None, tq, A_HEADS * HEAD_DIM), cur),
            pl.BlockSpec((None, BLOCK, kvw), prev),
            pl.BlockSpec((None, tq, kvw), cur),
            pl.BlockSpec((None, BLOCK, kvw), nxt),
            pl.BlockSpec((None, BLOCK, kvw), prev),
            pl.BlockSpec((None, tq, kvw), cur),
            pl.BlockSpec((None, BLOCK, kvw), nxt),
            _resident((A_HEADS, BLOCK, 3 * BLOCK), lambda bi, i: (0, 0, 0)),
        ],
        out_specs=pl.BlockSpec((None, tq, A_HEADS * HEAD_DIM), cur),
        out_shape=jax.ShapeDtypeStruct((b, seq, A_HEADS * HEAD_DIM), BF16),
        scratch_shapes=[pltpu.VMEM((tq + 2 * BLOCK, kvw), BF16),
                        pltpu.VMEM((tq + 2 * BLOCK, kvw), BF16)],
        compiler_params=_params(2),
        name="window_attn",
    )(sink, qa, ka, ka, ka, va, va, va, bias)


def _mla_kernel(q_ref, k_ref, v_ref, o_ref, *, tk, nk):
    q = q_ref[...]
    tq = q.shape[0]

    def body(c, carry):
        m, l, acc = carry
        off = pl.multiple_of(c * tk, tk)
        k = k_ref[pl.ds(off, tk), :]
        v = v_ref[pl.ds(off, tk), :]
        s = _dot_nt(q, k)
        m_new = jnp.maximum(m, jnp.max(s, axis=-1, keepdims=True))
        a = jnp.exp(m - m_new)
        p = jnp.exp(s - m_new)
        l = a * l + jnp.sum(p, axis=-1, keepdims=True)
        acc = a * acc + _dot(p.astype(BF16), v)
        return m_new, l, acc

    init = (jnp.full((tq, 1), -jnp.inf, F32), jnp.zeros((tq, 1), F32), jnp.zeros((tq, V_DIM), F32))
    _, l, acc = lax.fori_loop(0, nk, body, init)
    o_ref[...] = (acc / l).astype(o_ref.dtype)


def _mla_attention(qm, km, vm, tq=256, tk=512):
    b, seq, _ = qm.shape
    return pl.pallas_call(
        functools.partial(_mla_kernel, tk=tk, nk=seq // tk),
        grid=(b, B_HEADS, seq // tq),
        in_specs=[
            pl.BlockSpec((None, tq, QK_PAD), lambda bi, h, i: (bi, i, h)),
            pl.BlockSpec((None, seq, QK_PAD), lambda bi, h, i: (bi, 0, h)),
            pl.BlockSpec((None, seq, V_DIM), lambda bi, h, i: (bi, 0, h)),
        ],
        out_specs=pl.BlockSpec((None, tq, V_DIM), lambda bi, h, i: (bi, i, h)),
        out_shape=jax.ShapeDtypeStruct((b, seq, B_HEADS * V_DIM), BF16),
        compiler_params=_params(3),
        name="mla_attn",
    )(qm, km, vm)


def _memkv_kernel(m_ref, w_ref, o_ref):
    o_ref[...] = _dot(m_ref[...].astype(BF16), w_ref[...]).astype(BF16)


def _memory_kv(mem2, w_mem_kv):
    t = mem2.shape[0]
    n = w_mem_kv.shape[1]
    return pl.pallas_call(
        _memkv_kernel,
        grid=(t // N_MEM,),
        in_specs=[pl.BlockSpec((N_MEM, D_MODEL), lambda i: (i, 0)),
                  _resident((D_MODEL, n), lambda i: (0, 0))],
        out_specs=pl.BlockSpec((N_MEM, n), lambda i: (i, 0)),
        out_shape=jax.ShapeDtypeStruct((t, n), BF16),
        compiler_params=_params(1),
        name="mem_kv",
    )(mem2, w_mem_kv)


def _cross_kernel(q_ref, k_ref, v_ref, o_ref):
    for h in range(C_HEADS):
        sl = slice(h * HEAD_DIM, (h + 1) * HEAD_DIM)
        s = _dot_nt(q_ref[:, sl], k_ref[:, sl])
        m = jnp.max(s, axis=-1, keepdims=True)
        p = jnp.exp(s - m)
        den = jnp.sum(p, axis=-1, keepdims=True)
        o_ref[:, sl] = (_dot(p.astype(BF16), v_ref[:, sl]) / den).astype(BF16)


def _cross_attention(qc, mkv, tq=512):
    b, seq, w = qc.shape
    return pl.pallas_call(
        _cross_kernel,
        grid=(b, seq // tq),
        in_specs=[pl.BlockSpec((None, tq, w), lambda bi, i: (bi, i, 0)),
                  pl.BlockSpec((None, N_MEM, w), lambda bi, i: (bi, 0, 0)),
                  pl.BlockSpec((None, N_MEM, w), lambda bi, i: (bi, 0, 1))],
        out_specs=pl.BlockSpec((None, tq, w), lambda bi, i: (bi, i, 0)),
        out_shape=jax.ShapeDtypeStruct((b, seq, w), BF16),
        compiler_params=_params(2),
        name="cross_attn",
    )(qc, mkv, mkv)


def _merge_kernel(xb_ref, a_ref, b_ref, c_ref, wg0_ref, wg1_ref, wg2_ref, bg0_ref, bg1_ref, bg2_ref,
                  wa_ref, wb_ref, wc_ref, o_ref):
    xb = xb_ref[...]
    acc = jax.nn.sigmoid(_dot(xb, wg0_ref[...]) + bg0_ref[...]) * _dot(a_ref[...], wa_ref[...])
    acc += jax.nn.sigmoid(_dot(xb, wg1_ref[...]) + bg1_ref[...]) * _dot(b_ref[...], wb_ref[...])
    acc += jax.nn.sigmoid(_dot(xb, wg2_ref[...]) + bg2_ref[...]) * _dot(c_ref[...], wc_ref[...])
    o_ref[...] = acc.astype(BF16)


def _gated_merge(xb, a, b, c, w_gate, b_gate, w_br_a, w_br_b, w_br_c, tm=1024, tn=512):
    t = xb.shape[0]
    nj = D_MODEL // tn
    row = lambda i, j: (i, 0)
    colj = lambda i, j: (0, j)
    gate = lambda n: (lambda i, j: (0, n * nj + j))
    return pl.pallas_call(
        _merge_kernel,
        grid=(t // tm, nj),
        in_specs=[
            pl.BlockSpec((tm, D_MODEL), row),
            pl.BlockSpec((tm, a.shape[1]), row),
            pl.BlockSpec((tm, b.shape[1]), row),
            pl.BlockSpec((tm, c.shape[1]), row),
            pl.BlockSpec((D_MODEL, tn), gate(0)),
            pl.BlockSpec((D_MODEL, tn), gate(1)),
            pl.BlockSpec((D_MODEL, tn), gate(2)),
            pl.BlockSpec((1, tn), gate(0)),
            pl.BlockSpec((1, tn), gate(1)),
            pl.BlockSpec((1, tn), gate(2)),
            pl.BlockSpec((a.shape[1], tn), colj),
            pl.BlockSpec((b.shape[1], tn), colj),
            pl.BlockSpec((c.shape[1], tn), colj),
        ],
        out_specs=pl.BlockSpec((tm, tn), lambda i, j: (i, j)),
        out_shape=jax.ShapeDtypeStruct((t, D_MODEL), BF16),
        compiler_params=_params(2),
        name="gated_merge",
    )(xb, a, b, c, w_gate, w_gate, w_gate, b_gate, b_gate, b_gate, w_br_a, w_br_b, w_br_c)


def _resid_ln_kernel(u_ref, w_ref, r_ref, g_ref, b_ref, *o_refs):
    y = _layer_norm(ALPHA * r_ref[...] + _dot(u_ref[...], w_ref[...]), g_ref[...], b_ref[...])
    o_refs[0][...] = y
    if len(o_refs) > 1:
        o_refs[1][...] = y.astype(BF16)


def _proj_resid_ln(u, w, resid, g, b, tm, with_bf16):
    t, k = u.shape
    row = lambda i: (i, 0)
    fixed = lambda i: (0, 0)
    out_shape = [jax.ShapeDtypeStruct((t, D_MODEL), F32)]
    if with_bf16:
        out_shape.append(jax.ShapeDtypeStruct((t, D_MODEL), BF16))
    return pl.pallas_call(
        _resid_ln_kernel,
        grid=(t // tm,),
        in_specs=[pl.BlockSpec((tm, k), row),
                  _resident((k, D_MODEL), fixed),
                  pl.BlockSpec((tm, D_MODEL), row),
                  _resident((1, D_MODEL), fixed),
                  _resident((1, D_MODEL), fixed)],
        out_specs=[pl.BlockSpec((tm, D_MODEL), row) for _ in out_shape],
        out_shape=out_shape,
        compiler_params=_params(1),
        name="proj_resid_ln",
    )(u, w, resid, g, b)


def _swiglu_kernel(h_ref, wg_ref, wu_ref, o_ref):
    h = h_ref[...]
    gate = _dot(h, wg_ref[...])
    up = _dot(h, wu_ref[...])
    o_ref[...] = (gate * jax.nn.sigmoid(gate) * up).astype(BF16)


def _swiglu_in(hb, w_ffn_in, tm=1024, tn=512):
    t = hb.shape[0]
    nj = D_FF // tn
    return pl.pallas_call(
        _swiglu_kernel,
        grid=(t // tm, nj),
        in_specs=[pl.BlockSpec((tm, D_MODEL), lambda i, j: (i, 0)),
                  pl.BlockSpec((D_MODEL, tn), lambda i, j: (0, j)),
                  pl.BlockSpec((D_MODEL, tn), lambda i, j: (0, nj + j))],
        out_specs=pl.BlockSpec((tm, tn), lambda i, j: (i, j)),
        out_shape=jax.ShapeDtypeStruct((t, D_FF), BF16),
        compiler_params=_params(2),
        name="swiglu_in",
    )(hb, w_ffn_in, w_ffn_in)


def _rope_tables(seq):
    inv = 1.0 / (ROPE_THETA ** (jnp.arange(ROPE_HALF, dtype=jnp.float32) / ROPE_HALF))
    ang = jnp.arange(seq, dtype=jnp.float32)[:, None] * inv[None, :]
    cos, sin = jnp.cos(ang), jnp.sin(ang)
    z = jnp.zeros_like(cos)
    return (jnp.concatenate([cos, cos, z, z], axis=-1),
            jnp.concatenate([-sin, z, z, z], axis=-1),
            jnp.concatenate([z, sin, z, z], axis=-1))


def _prepare_weights(w_in, w_uq, w_ukv):
    parts = np.cumsum((A_HEADS * HEAD_DIM, A_KV_HEADS * HEAD_DIM, A_KV_HEADS * HEAD_DIM, Q_LORA, KV_LORA, QK_ROPE))
    qa, ka, va, cq, ckv, kr, qc = jnp.split(w_in, [int(p) for p in parts], axis=-1)
    pad = jnp.zeros((D_MODEL, 128 - QK_ROPE), w_in.dtype)
    w_in_p = jnp.concatenate([qa, ka, va, cq, ckv, qc, kr, pad], axis=-1).astype(BF16)
    w_uq_p = jnp.pad(w_uq.reshape(Q_LORA, B_HEADS, QK_NOPE + QK_ROPE),
                     ((0, 0), (0, 0), (0, QK_PAD - QK_NOPE - QK_ROPE))).reshape(Q_LORA, B_HEADS * QK_PAD).astype(BF16)
    kv = w_ukv.reshape(KV_LORA, B_HEADS, 2, QK_NOPE)
    w_ukv_p = jnp.concatenate([kv[:, :, 0].reshape(KV_LORA, -1), kv[:, :, 1].reshape(KV_LORA, -1)],
                              axis=-1).astype(BF16)
    return w_in_p, w_uq_p, w_ukv_p


def _encoder_layer(x, mem, bias, w):
    b, seq, d = x.shape
    t = b * seq
    x2 = x.reshape(t, d)
    rope_c, rope_sa, rope_sb = _rope_tables(seq)
    xb, qa, ka, va, qc, qm, km, vm = _project(
        x2, w["w_in"], w["q_norm_g"], w["w_uq"], w["kv_norm_g"], w["w_ukv"], rope_c, rope_sa, rope_sb, seq)
    r3 = lambda v: v.reshape(b, seq, v.shape[-1])
    a_out = _window_attention(r3(qa), r3(ka), r3(va), bias, w["sink"])
    b_out = _mla_attention(r3(qm), r3(km), r3(vm))
    mkv = _memory_kv(mem.reshape(b * N_MEM, d), w["w_mem_kv"]).reshape(b, N_MEM, -1)
    c_out = _cross_attention(r3(qc), mkv)
    merged = _gated_merge(xb, a_out.reshape(t, -1), b_out.reshape(t, -1), c_out.reshape(t, -1),
                          w["w_gate"], w["b_gate"], w["w_br_a"], w["w_br_b"], w["w_br_c"])
    h, hb = _proj_resid_ln(merged, w["w_o"], x2, w["ln1_g"], w["ln1_b"], tm=512, with_bf16=True)
    act = _swiglu_in(hb, w["w_ffn_in"])
    (y,) = _proj_resid_ln(act, w["w_ffn_down"], h, w["ln2_g"], w["ln2_b"], tm=256, with_bf16=False)
    return y.reshape(b, seq, d)


def kernel(x_prompt, x_sample, mem_prompt, mem_sample, w_in, rel_bias, sink, q_norm_g, w_uq, kv_norm_g, w_ukv,
           w_mem_kv, w_gate, b_gate, w_br_a, w_br_b, w_br_c, w_o, ln1_g, ln1_b, w_ffn_in, w_ffn_down, ln2_g, ln2_b):
    bias = _window_bias(rel_bias)
    xp, xs = x_prompt, x_sample
    for l in range(DEPTH):
        w_in_p, w_uq_p, w_ukv_p = _prepare_weights(w_in[l], w_uq[l], w_ukv[l])
        w = dict(
            w_in=w_in_p, w_uq=w_uq_p, w_ukv=w_ukv_p, sink=sink[l],
            q_norm_g=q_norm_g[l][None, :], kv_norm_g=kv_norm_g[l][None, :],
            w_mem_kv=w_mem_kv[l].astype(BF16), w_gate=w_gate[l].astype(BF16), b_gate=b_gate[l][None, :],
            w_br_a=w_br_a[l].astype(BF16), w_br_b=w_br_b[l].astype(BF16), w_br_c=w_br_c[l].astype(BF16),
            w_o=w_o[l].astype(BF16), ln1_g=ln1_g[l][None, :], ln1_b=ln1_b[l][None, :],
            w_ffn_in=w_ffn_in[l].astype(BF16), w_ffn_down=w_ffn_down[l].astype(BF16),
            ln2_g=ln2_g[l][None, :], ln2_b=ln2_b[l][None, :])
        xp = _encoder_layer(xp, mem_prompt, bias, w)
        xs = _encoder_layer(xs, mem_sample, bias, w)
    return (xp, xs)
```

```python
import functools
import math

import jax
import jax.numpy as jnp
import numpy as np
from jax import lax
from jax.experimental import pallas as pl
from jax.experimental.pallas import tpu as pltpu

F32 = jnp.float32
BF16 = jnp.bfloat16

D_MODEL = 2048
HEAD_DIM = 128
A_HEADS = 6
A_KV_HEADS = 2
A_GROUP = A_HEADS // A_KV_HEADS
WINDOW = 128
BLOCK = 128
N_BUCKETS = 32
MAX_DISTANCE = 128
B_HEADS = 6
Q_LORA = 512
KV_LORA = 512
QK_NOPE = 128
QK_ROPE = 64
V_DIM = 128
ROPE_THETA = 10000.0
C_HEADS = 4
N_MEM = 256
N_BRANCH = 3
D_FF = -(-8 * D_MODEL // (3 * 256)) * 256
DEPTH = 1
ALPHA = (2 * DEPTH) ** 0.25
LN_EPS = 1e-5
RMS_EPS = 1e-6
NEG = -1e30

QK_PAD = 256
ROPE_HALF = QK_ROPE // 2
VMEM_LIMIT = 56 * 1024 * 1024

_OFF_QA = 0
_OFF_KA = _OFF_QA + A_HEADS * HEAD_DIM
_OFF_VA = _OFF_KA + A_KV_HEADS * HEAD_DIM
_OFF_CQ = _OFF_VA + A_KV_HEADS * HEAD_DIM
_OFF_CKV = _OFF_CQ + Q_LORA
_OFF_QC = _OFF_CKV + KV_LORA
_OFF_KR = _OFF_QC + C_HEADS * HEAD_DIM
D_IN_PAD = _OFF_KR + 128


def _params(n_axes):
    return pltpu.CompilerParams(dimension_semantics=("arbitrary",) * n_axes,
                                vmem_limit_bytes=VMEM_LIMIT)


def _resident(shape, index_map):
    return pl.BlockSpec(shape, index_map, pipeline_mode=pl.Buffered(1))


def _dot(a, b):
    return jnp.dot(a, b, preferred_element_type=F32)


def _dot_nt(a, b):
    return lax.dot_general(a, b, (((1,), (1,)), ((), ())), preferred_element_type=F32)


def _layer_norm(v, g, b):
    mu = jnp.mean(v, axis=-1, keepdims=True)
    c = v - mu
    var = jnp.mean(c * c, axis=-1, keepdims=True)
    return c * lax.rsqrt(var + LN_EPS) * g + b


def _rms_norm(v, g):
    ms = jnp.mean(v * v, axis=-1, keepdims=True)
    return v * lax.rsqrt(ms + RMS_EPS) * g


def _rope_tile(t, c, sa, sb):
    return t * c + pltpu.roll(t, 128 - ROPE_HALF, 1) * sa + pltpu.roll(t, ROPE_HALF, 1) * sb


def _proj_kernel(x_ref, w_in_ref, qg_ref, w_uq_ref, kvg_ref, w_ukv_ref, c_ref, sa_ref, sb_ref,
                 xb_ref, qa_ref, ka_ref, va_ref, qc_ref, qm_ref, km_ref, vm_ref):
    xb = x_ref[...].astype(BF16)
    xb_ref[...] = xb
    proj = _dot(xb, w_in_ref[...])
    qa_ref[...] = (proj[:, _OFF_QA:_OFF_KA] * (HEAD_DIM ** -0.5)).astype(BF16)
    ka_ref[...] = proj[:, _OFF_KA:_OFF_VA].astype(BF16)
    va_ref[...] = proj[:, _OFF_VA:_OFF_CQ].astype(BF16)
    qc_ref[...] = (proj[:, _OFF_QC:_OFF_KR] * (HEAD_DIM ** -0.5)).astype(BF16)

    c, sa, sb = c_ref[...], sa_ref[...], sb_ref[...]
    cqn = _rms_norm(proj[:, _OFF_CQ:_OFF_CKV], qg_ref[...]).astype(BF16)
    qb = _dot(cqn, w_uq_ref[...])
    ckvn = _rms_norm(proj[:, _OFF_CKV:_OFF_QC], kvg_ref[...]).astype(BF16)
    kvb = _dot(ckvn, w_ukv_ref[...])
    kr = _rope_tile(proj[:, _OFF_KR:D_IN_PAD], c, sa, sb).astype(BF16)
    scale = (QK_NOPE + QK_ROPE) ** -0.5 * math.log2(math.e)
    for h in range(B_HEADS):
        lo = h * QK_PAD
        qm_ref[:, lo:lo + QK_NOPE] = (qb[:, lo:lo + QK_NOPE] * scale).astype(BF16)
        qm_ref[:, lo + QK_NOPE:lo + QK_PAD] = (
            _rope_tile(qb[:, lo + QK_NOPE:lo + QK_PAD], c, sa, sb) * scale).astype(BF16)
        km_ref[:, lo:lo + QK_NOPE] = kvb[:, h * QK_NOPE:(h + 1) * QK_NOPE].astype(BF16)
        km_ref[:, lo + QK_NOPE:lo + QK_PAD] = kr
    vm_ref[...] = kvb[:, B_HEADS * QK_NOPE:].astype(BF16)


def _project(x2, w_in, q_norm_g, w_uq, kv_norm_g, w_ukv, rope_c, rope_sa, rope_sb, seq, tm=256):
    t = x2.shape[0]
    n_pos = seq // tm
    row = lambda i: (i, 0)
    fixed = lambda i: (0, 0)
    pos = lambda i: (i % n_pos, 0)
    widths = (D_MODEL, A_HEADS * HEAD_DIM, A_KV_HEADS * HEAD_DIM, A_KV_HEADS * HEAD_DIM,
              C_HEADS * HEAD_DIM, B_HEADS * QK_PAD, B_HEADS * QK_PAD, B_HEADS * V_DIM)
    return pl.pallas_call(
        _proj_kernel,
        grid=(t // tm,),
        in_specs=[
            pl.BlockSpec((tm, D_MODEL), row),
            _resident((D_MODEL, D_IN_PAD), fixed),
            _resident((1, Q_LORA), fixed),
            _resident((Q_LORA, B_HEADS * QK_PAD), fixed),
            _resident((1, KV_LORA), fixed),
            _resident((KV_LORA, B_HEADS * (QK_NOPE + V_DIM)), fixed),
            pl.BlockSpec((tm, 128), pos),
            pl.BlockSpec((tm, 128), pos),
            pl.BlockSpec((tm, 128), pos),
        ],
        out_specs=[pl.BlockSpec((tm, w), row) for w in widths],
        out_shape=[jax.ShapeDtypeStruct((t, w), BF16) for w in widths],
        compiler_params=_params(1),
        name="proj",
    )(x2, w_in, q_norm_g, w_uq, kv_norm_g, w_ukv, rope_c, rope_sa, rope_sb)


def _bias_kernel(rb_ref, bucket_ref, band_ref, o_ref):
    bucket = bucket_ref[...]
    band = band_ref[...] != 0
    for h in range(A_HEADS):
        acc = jnp.zeros(bucket.shape, F32)
        for b in range(N_BUCKETS):
            acc = jnp.where(bucket == b, rb_ref[b, h], acc)
        o_ref[h] = jnp.where(band, acc, NEG)


def _t5_bucket(rel):
    half = N_BUCKETS // 2
    max_exact = half // 2
    ret = (rel > 0).astype(jnp.int32) * half
    n = jnp.abs(rel)
    large = max_exact + (jnp.log(jnp.maximum(n, 1).astype(jnp.float32) / max_exact)
                         / math.log(MAX_DISTANCE / max_exact) * (half - max_exact)).astype(jnp.int32)
    large = jnp.minimum(large, half - 1)
    return ret + jnp.where(n < max_exact, n, large)


def _window_bias(rel_bias):
    rel = (jnp.arange(3 * BLOCK) - BLOCK)[None, :] - jnp.arange(BLOCK)[:, None]
    bucket = _t5_bucket(rel).astype(jnp.int32)
    band = (jnp.abs(rel) <= WINDOW).astype(jnp.int32)
    return pl.pallas_call(
        _bias_kernel,
        in_specs=[pl.BlockSpec(memory_space=pltpu.SMEM),
                  pl.BlockSpec(memory_space=pltpu.VMEM),
                  pl.BlockSpec(memory_space=pltpu.VMEM)],
        out_specs=pl.BlockSpec(memory_space=pltpu.VMEM),
        out_shape=jax.ShapeDtypeStruct((A_HEADS, BLOCK, 3 * BLOCK), F32),
        name="win_bias",
    )(rel_bias, bucket, band)


def _win_kernel(sink_ref, q_ref, kp_ref, kc_ref, kn_ref, vp_ref, vc_ref, vn_ref, bias_ref, o_ref,
                kbuf, vbuf, *, tq, seq):
    i = pl.program_id(1)
    kbuf[0:BLOCK] = kp_ref[...]
    kbuf[BLOCK:BLOCK + tq] = kc_ref[...]
    kbuf[BLOCK + tq:] = kn_ref[...]
    vbuf[0:BLOCK] = vp_ref[...]
    vbuf[BLOCK:BLOCK + tq] = vc_ref[...]
    vbuf[BLOCK + tq:] = vn_ref[...]
    rows = A_GROUP * BLOCK
    col = lax.broadcasted_iota(jnp.int32, (rows, 3 * BLOCK), 1)
    row_head = lax.broadcasted_iota(jnp.int32, (rows, 1), 0) // BLOCK
    for g in range(A_KV_HEADS):
        sink = jnp.zeros((rows, 1), F32)
        for r in range(A_GROUP):
            sink = jnp.where(row_head == r, sink_ref[g * A_GROUP + r], sink)
        bias = bias_ref[g * A_GROUP:(g + 1) * A_GROUP].reshape(rows, 3 * BLOCK)
        for j in range(tq // BLOCK):
            kpos = col + (i * tq + (j - 1) * BLOCK)
            valid = (kpos >= 0) & (kpos < seq)
            kw = kbuf[j * BLOCK:(j + 3) * BLOCK, g * HEAD_DIM:(g + 1) * HEAD_DIM]
            vw = vbuf[j * BLOCK:(j + 3) * BLOCK, g * HEAD_DIM:(g + 1) * HEAD_DIM]
            q = jnp.concatenate(
                [q_ref[j * BLOCK:(j + 1) * BLOCK, (g * A_GROUP + r) * HEAD_DIM:(g * A_GROUP + r + 1) * HEAD_DIM]
                 for r in range(A_GROUP)], axis=0)
            s = jnp.where(valid, _dot_nt(q, kw) + bias, NEG)
            m = jnp.maximum(jnp.max(s, axis=-1, keepdims=True), sink)
            p = jnp.exp(s - m)
            den = jnp.sum(p, axis=-1, keepdims=True) + jnp.exp(sink - m)
            o = _dot(p.astype(BF16), vw) / den
            for r in range(A_GROUP):
                h = g * A_GROUP + r
                o_ref[j * BLOCK:(j + 1) * BLOCK, h * HEAD_DIM:(h + 1) * HEAD_DIM] = (
                    o[r * BLOCK:(r + 1) * BLOCK].astype(BF16))


def _window_attention(qa, ka, va, bias, sink, tq=512):
    b, seq, _ = qa.shape
    nb = seq // BLOCK
    per = tq // BLOCK
    kvw = A_KV_HEADS * HEAD_DIM
    prev = lambda bi, i: (bi, jnp.maximum(i * per - 1, 0), 0)
    cur = lambda bi, i: (bi, i, 0)
    nxt = lambda bi, i: (bi, jnp.minimum((i + 1) * per, nb - 1), 0)
    return pl.pallas_call(
        functools.partial(_win_kernel, tq=tq, seq=seq),
        grid=(b, seq // tq),
        in_specs=[
            pl.BlockSpec(memory_space=pltpu.SMEM),
            pl.BlockSpec((None, tq, A_HEADS * HEAD_DIM), cur),
            pl.BlockSpec((None, BLOCK, kvw), prev),
            pl.BlockSpec((None, tq, kvw), cur),
            pl.BlockSpec((None, BLOCK, kvw), nxt),
            pl.BlockSpec((None, BLOCK, kvw), prev),
            pl.BlockSpec((None, tq, kvw), cur),
            pl.BlockSpec((None, BLOCK, kvw), nxt),
            _resident((A_HEADS, BLOCK, 3 * BLOCK), lambda bi, i: (0, 0, 0)),
        ],
        out_specs=pl.BlockSpec((None, tq, A_HEADS * HEAD_DIM), cur),
        out_shape=jax.ShapeDtypeStruct((b, seq, A_HEADS * HEAD_DIM), BF16),
        scratch_shapes=[pltpu.VMEM((tq + 2 * BLOCK, kvw), BF16),
                        pltpu.VMEM((tq + 2 * BLOCK, kvw), BF16)],
        compiler_params=_params(2),
        name="window_attn",
    )(sink, qa, ka, ka, ka, va, va, va, bias)


def _mla_kernel(q_ref, k_ref, vt_ref, o_ref, s_buf, *, tk, nk, unroll):
    q = q_ref[...]
    tq = q.shape[0]
    grp = tk // 8

    def scores(c, mx):
        off = pl.multiple_of(c * tk, tk)
        st = _dot_nt(k_ref[pl.ds(off, tk), :], q)
        s_buf[pl.ds(off, tk), :] = st
        return jnp.maximum(mx, jnp.max(st.reshape(grp, 8, tq), axis=0))

    mx = lax.fori_loop(0, nk, scores, jnp.full((8, tq), -jnp.inf, F32), unroll=unroll)
    m8 = jnp.broadcast_to(jnp.max(mx, axis=0, keepdims=True), (8, tq))

    def weighted(c, carry):
        acc, ls = carry
        off = pl.multiple_of(c * tk, tk)
        p = jnp.exp2(s_buf[pl.ds(off, tk), :].reshape(grp, 8, tq) - m8[None])
        ls = ls + jnp.sum(p, axis=0)
        acc = acc + _dot(vt_ref[:, pl.ds(off, tk)], p.reshape(tk, tq).astype(BF16))
        return acc, ls

    acc, ls = lax.fori_loop(0, nk, weighted, (jnp.zeros((V_DIM, tq), F32), jnp.zeros((8, tq), F32)),
                            unroll=unroll)
    o_t = acc / jnp.sum(ls, axis=0, keepdims=True)
    o_ref[...] = o_t.T.astype(o_ref.dtype)


def _mla_attention(qm, km, vt, tq=512, tk=2048, unroll=True):
    b, seq, _ = qm.shape
    return pl.pallas_call(
        functools.partial(_mla_kernel, tk=tk, nk=seq // tk, unroll=unroll),
        grid=(b, B_HEADS, seq // tq),
        in_specs=[
            pl.BlockSpec((None, tq, QK_PAD), lambda bi, h, i: (bi, i, h)),
            pl.BlockSpec((None, seq, QK_PAD), lambda bi, h, i: (bi, 0, h), pipeline_mode=pl.Buffered(1)),
            pl.BlockSpec((None, None, V_DIM, seq), lambda bi, h, i: (bi, h, 0, 0), pipeline_mode=pl.Buffered(1)),
        ],
        out_specs=pl.BlockSpec((None, tq, V_DIM), lambda bi, h, i: (bi, i, h)),
        out_shape=jax.ShapeDtypeStruct((b, seq, B_HEADS * V_DIM), BF16),
        scratch_shapes=[pltpu.VMEM((seq, tq), F32)],
        compiler_params=_params(3),
        name="mla_attn",
    )(qm, km, vt)


def _memkv_kernel(m_ref, w_ref, o_ref):
    o_ref[...] = _dot(m_ref[...].astype(BF16), w_ref[...]).astype(BF16)


def _memory_kv(mem2, w_mem_kv):
    t = mem2.shape[0]
    n = w_mem_kv.shape[1]
    return pl.pallas_call(
        _memkv_kernel,
        grid=(t // N_MEM,),
        in_specs=[pl.BlockSpec((N_MEM, D_MODEL), lambda i: (i, 0)),
                  _resident((D_MODEL, n), lambda i: (0, 0))],
        out_specs=pl.BlockSpec((N_MEM, n), lambda i: (i, 0)),
        out_shape=jax.ShapeDtypeStruct((t, n), BF16),
        compiler_params=_params(1),
        name="mem_kv",
    )(mem2, w_mem_kv)


def _cross_kernel(q_ref, k_ref, v_ref, o_ref):
    for h in range(C_HEADS):
        sl = slice(h * HEAD_DIM, (h + 1) * HEAD_DIM)
        s = _dot_nt(q_ref[:, sl], k_ref[:, sl])
        m = jnp.max(s, axis=-1, keepdims=True)
        p = jnp.exp(s - m)
        den = jnp.sum(p, axis=-1, keepdims=True)
        o_ref[:, sl] = (_dot(p.astype(BF16), v_ref[:, sl]) / den).astype(BF16)


def _cross_attention(qc, mkv, tq=512):
    b, seq, w = qc.shape
    return pl.pallas_call(
        _cross_kernel,
        grid=(b, seq // tq),
        in_specs=[pl.BlockSpec((None, tq, w), lambda bi, i: (bi, i, 0)),
                  pl.BlockSpec((None, N_MEM, w), lambda bi, i: (bi, 0, 0)),
                  pl.BlockSpec((None, N_MEM, w), lambda bi, i: (bi, 0, 1))],
        out_specs=pl.BlockSpec((None, tq, w), lambda bi, i: (bi, i, 0)),
        out_shape=jax.ShapeDtypeStruct((b, seq, w), BF16),
        compiler_params=_params(2),
        name="cross_attn",
    )(qc, mkv, mkv)


def _merge_kernel(xb_ref, a_ref, b_ref, c_ref, wg0_ref, wg1_ref, wg2_ref, bg0_ref, bg1_ref, bg2_ref,
                  wa_ref, wb_ref, wc_ref, o_ref):
    xb = xb_ref[...]
    acc = jax.nn.sigmoid(_dot(xb, wg0_ref[...]) + bg0_ref[...]) * _dot(a_ref[...], wa_ref[...])
    acc += jax.nn.sigmoid(_dot(xb, wg1_ref[...]) + bg1_ref[...]) * _dot(b_ref[...], wb_ref[...])
    acc += jax.nn.sigmoid(_dot(xb, wg2_ref[...]) + bg2_ref[...]) * _dot(c_ref[...], wc_ref[...])
    o_ref[...] = acc.astype(BF16)


def _gated_merge(xb, a, b, c, w_gate, b_gate, w_br_a, w_br_b, w_br_c, tm=1024, tn=512):
    t = xb.shape[0]
    nj = D_MODEL // tn
    row = lambda i, j: (i, 0)
    colj = lambda i, j: (0, j)
    gate = lambda n: (lambda i, j: (0, n * nj + j))
    return pl.pallas_call(
        _merge_kernel,
        grid=(t // tm, nj),
        in_specs=[
            pl.BlockSpec((tm, D_MODEL), row),
            pl.BlockSpec((tm, a.shape[1]), row),
            pl.BlockSpec((tm, b.shape[1]), row),
            pl.BlockSpec((tm, c.shape[1]), row),
            pl.BlockSpec((D_MODEL, tn), gate(0)),
            pl.BlockSpec((D_MODEL, tn), gate(1)),
            pl.BlockSpec((D_MODEL, tn), gate(2)),
            pl.BlockSpec((1, tn), gate(0)),
            pl.BlockSpec((1, tn), gate(1)),
            pl.BlockSpec((1, tn), gate(2)),
            pl.BlockSpec((a.shape[1], tn), colj),
            pl.BlockSpec((b.shape[1], tn), colj),
            pl.BlockSpec((c.shape[1], tn), colj),
        ],
        out_specs=pl.BlockSpec((tm, tn), lambda i, j: (i, j)),
        out_shape=jax.ShapeDtypeStruct((t, D_MODEL), BF16),
        compiler_params=_params(2),
        name="gated_merge",
    )(xb, a, b, c, w_gate, w_gate, w_gate, b_gate, b_gate, b_gate, w_br_a, w_br_b, w_br_c)


def _resid_ln_kernel(u_ref, w_ref, r_ref, g_ref, b_ref, *o_refs):
    y = _layer_norm(ALPHA * r_ref[...] + _dot(u_ref[...], w_ref[...]), g_ref[...], b_ref[...])
    o_refs[0][...] = y
    if len(o_refs) > 1:
        o_refs[1][...] = y.astype(BF16)


def _proj_resid_ln(u, w, resid, g, b, tm, with_bf16):
    t, k = u.shape
    row = lambda i: (i, 0)
    fixed = lambda i: (0, 0)
    out_shape = [jax.ShapeDtypeStruct((t, D_MODEL), F32)]
    if with_bf16:
        out_shape.append(jax.ShapeDtypeStruct((t, D_MODEL), BF16))
    return pl.pallas_call(
        _resid_ln_kernel,
        grid=(t // tm,),
        in_specs=[pl.BlockSpec((tm, k), row),
                  _resident((k, D_MODEL), fixed),
                  pl.BlockSpec((tm, D_MODEL), row),
                  _resident((1, D_MODEL), fixed),
                  _resident((1, D_MODEL), fixed)],
        out_specs=[pl.BlockSpec((tm, D_MODEL), row) for _ in out_shape],
        out_shape=out_shape,
        compiler_params=_params(1),
        name="proj_resid_ln",
    )(u, w, resid, g, b)


def _swiglu_kernel(h_ref, wg_ref, wu_ref, o_ref):
    h = h_ref[...]
    gate = _dot(h, wg_ref[...])
    up = _dot(h, wu_ref[...])
    o_ref[...] = (gate * jax.nn.sigmoid(gate) * up).astype(BF16)


def _swiglu_in(hb, w_ffn_in, tm=1024, tn=512):
    t = hb.shape[0]
    nj = D_FF // tn
    return pl.pallas_call(
        _swiglu_kernel,
        grid=(t // tm, nj),
        in_specs=[pl.BlockSpec((tm, D_MODEL), lambda i, j: (i, 0)),
                  pl.BlockSpec((D_MODEL, tn), lambda i, j: (0, j)),
                  pl.BlockSpec((D_MODEL, tn), lambda i, j: (0, nj + j))],
        out_specs=pl.BlockSpec((tm, tn), lambda i, j: (i, j)),
        out_shape=jax.ShapeDtypeStruct((t, D_FF), BF16),
        compiler_params=_params(2),
        name="swiglu_in",
    )(hb, w_ffn_in, w_ffn_in)


def _rope_tables(seq):
    inv = 1.0 / (ROPE_THETA ** (jnp.arange(ROPE_HALF, dtype=jnp.float32) / ROPE_HALF))
    ang = jnp.arange(seq, dtype=jnp.float32)[:, None] * inv[None, :]
    cos, sin = jnp.cos(ang), jnp.sin(ang)
    z = jnp.zeros_like(cos)
    return (jnp.concatenate([cos, cos, z, z], axis=-1),
            jnp.concatenate([-sin, z, z, z], axis=-1),
            jnp.concatenate([z, sin, z, z], axis=-1))


def _prepare_weights(w_in, w_uq, w_ukv):
    parts = np.cumsum((A_HEADS * HEAD_DIM, A_KV_HEADS * HEAD_DIM, A_KV_HEADS * HEAD_DIM, Q_LORA, KV_LORA, QK_ROPE))
    qa, ka, va, cq, ckv, kr, qc = jnp.split(w_in, [int(p) for p in parts], axis=-1)
    pad = jnp.zeros((D_MODEL, 128 - QK_ROPE), w_in.dtype)
    w_in_p = jnp.concatenate([qa, ka, va, cq, ckv, qc, kr, pad], axis=-1).astype(BF16)
    w_uq_p = jnp.pad(w_uq.reshape(Q_LORA, B_HEADS, QK_NOPE + QK_ROPE),
                     ((0, 0), (0, 0), (0, QK_PAD - QK_NOPE - QK_ROPE))).reshape(Q_LORA, B_HEADS * QK_PAD).astype(BF16)
    kv = w_ukv.reshape(KV_LORA, B_HEADS, 2, QK_NOPE)
    w_ukv_p = jnp.concatenate([kv[:, :, 0].reshape(KV_LORA, -1), kv[:, :, 1].reshape(KV_LORA, -1)],
                              axis=-1).astype(BF16)
    return w_in_p, w_uq_p, w_ukv_p


def _encoder_layer(x, mem, bias, w):
    b, seq, d = x.shape
    t = b * seq
    x2 = x.reshape(t, d)
    rope_c, rope_sa, rope_sb = _rope_tables(seq)
    xb, qa, ka, va, qc, qm, km, vm = _project(
        x2, w["w_in"], w["q_norm_g"], w["w_uq"], w["kv_norm_g"], w["w_ukv"], rope_c, rope_sa, rope_sb, seq)
    r3 = lambda v: v.reshape(b, seq, v.shape[-1])
    a_out = _window_attention(r3(qa), r3(ka), r3(va), bias, w["sink"])
    vt = vm.reshape(b, seq, B_HEADS, V_DIM).transpose(0, 2, 3, 1)
    b_out = _mla_attention(r3(qm), r3(km), vt)
    mkv = _memory_kv(mem.reshape(b * N_MEM, d), w["w_mem_kv"]).reshape(b, N_MEM, -1)
    c_out = _cross_attention(r3(qc), mkv)
    merged = _gated_merge(xb, a_out.reshape(t, -1), b_out.reshape(t, -1), c_out.reshape(t, -1),
                          w["w_gate"], w["b_gate"], w["w_br_a"], w["w_br_b"], w["w_br_c"])
    h, hb = _proj_resid_ln(merged, w["w_o"], x2, w["ln1_g"], w["ln1_b"], tm=512, with_bf16=True)
    act = _swiglu_in(hb, w["w_ffn_in"])
    (y,) = _proj_resid_ln(act, w["w_ffn_down"], h, w["ln2_g"], w["ln2_b"], tm=256, with_bf16=False)
    return y.reshape(b, seq, d)


def kernel(x_prompt, x_sample, mem_prompt, mem_sample, w_in, rel_bias, sink, q_norm_g, w_uq, kv_norm_g, w_ukv,
           w_mem_kv, w_gate, b_gate, w_br_a, w_br_b, w_br_c, w_o, ln1_g, ln1_b, w_ffn_in, w_ffn_down, ln2_g, ln2_b):
    bias = _window_bias(rel_bias)
    xp, xs = x_prompt, x_sample
    for l in range(DEPTH):
        w_in_p, w_uq_p, w_ukv_p = _prepare_weights(w_in[l], w_uq[l], w_ukv[l])
        w = dict(
            w_in=w_in_p, w_uq=w_uq_p, w_ukv=w_ukv_p, sink=sink[l],
            q_norm_g=q_norm_g[l][None, :], kv_norm_g=kv_norm_g[l][None, :],
            w_mem_kv=w_mem_kv[l].astype(BF16), w_gate=w_gate[l].astype(BF16), b_gate=b_gate[l][None, :],
            w_br_a=w_br_a[l].astype(BF16), w_br_b=w_br_b[l].astype(BF16), w_br_c=w_br_c[l].astype(BF16),
            w_o=w_o[l].astype(BF16), ln1_g=ln1_g[l][None, :], ln1_b=ln1_b[l][None, :],
            w_ffn_in=w_ffn_in[l].astype(BF16), w_ffn_down=w_ffn_down[l].astype(BF16),
            ln2_g=ln2_g[l][None, :], ln2_b=ln2_b[l][None, :])
        xp = _encoder_layer(xp, mem_prompt, bias, w)
        xs = _encoder_layer(xs, mem_sample, bias, w)
    return (xp, xs)
```

```python
import functools
import math

import jax
import jax.numpy as jnp
import numpy as np
from jax import lax
from jax.experimental import pallas as pl
from jax.experimental.pallas import tpu as pltpu

F32 = jnp.float32
BF16 = jnp.bfloat16

D_MODEL = 2048
HEAD_DIM = 128
A_HEADS = 6
A_KV_HEADS = 2
A_GROUP = A_HEADS // A_KV_HEADS
WINDOW = 128
BLOCK = 128
N_BUCKETS = 32
MAX_DISTANCE = 128
B_HEADS = 6
Q_LORA = 512
KV_LORA = 512
QK_NOPE = 128
QK_ROPE = 64
V_DIM = 128
ROPE_THETA = 10000.0
C_HEADS = 4
N_MEM = 256
N_BRANCH = 3
D_FF = -(-8 * D_MODEL // (3 * 256)) * 256
DEPTH = 1
ALPHA = (2 * DEPTH) ** 0.25
LN_EPS = 1e-5
RMS_EPS = 1e-6
NEG = -1e30

QK_PAD = 256
ROPE_HALF = QK_ROPE // 2
V_ROWS = V_DIM + 16
VMEM_LIMIT = 56 * 1024 * 1024

_OFF_QA = 0
_OFF_KA = _OFF_QA + A_HEADS * HEAD_DIM
_OFF_VA = _OFF_KA + A_KV_HEADS * HEAD_DIM
_OFF_CQ = _OFF_VA + A_KV_HEADS * HEAD_DIM
_OFF_CKV = _OFF_CQ + Q_LORA
_OFF_QC = _OFF_CKV + KV_LORA
_OFF_KR = _OFF_QC + C_HEADS * HEAD_DIM
D_IN_PAD = _OFF_KR + 128


def _params(n_axes):
    return pltpu.CompilerParams(dimension_semantics=("arbitrary",) * n_axes,
                                vmem_limit_bytes=VMEM_LIMIT)


def _resident(shape, index_map):
    return pl.BlockSpec(shape, index_map, pipeline_mode=pl.Buffered(1))


def _dot(a, b):
    return jnp.dot(a, b, preferred_element_type=F32)


def _dot_nt(a, b):
    return lax.dot_general(a, b, (((1,), (1,)), ((), ())), preferred_element_type=F32)


def _layer_norm(v, g, b):
    mu = jnp.mean(v, axis=-1, keepdims=True)
    c = v - mu
    var = jnp.mean(c * c, axis=-1, keepdims=True)
    return c * lax.rsqrt(var + LN_EPS) * g + b


def _rms_norm(v, g):
    ms = jnp.mean(v * v, axis=-1, keepdims=True)
    return v * lax.rsqrt(ms + RMS_EPS) * g


def _rope_tile(t, c, sa, sb):
    return t * c + pltpu.roll(t, 128 - ROPE_HALF, 1) * sa + pltpu.roll(t, ROPE_HALF, 1) * sb


def _proj_kernel(x_ref, w_in_ref, qg_ref, w_uq_ref, kvg_ref, w_ukv_ref, c_ref, sa_ref, sb_ref,
                 xb_ref, qa_ref, ka_ref, va_ref, qc_ref, qm_ref, km_ref, vm_ref):
    xb = x_ref[...].astype(BF16)
    xb_ref[...] = xb
    proj = _dot(xb, w_in_ref[...])
    qa_ref[...] = (proj[:, _OFF_QA:_OFF_KA] * (HEAD_DIM ** -0.5)).astype(BF16)
    ka_ref[...] = proj[:, _OFF_KA:_OFF_VA].astype(BF16)
    va_ref[...] = proj[:, _OFF_VA:_OFF_CQ].astype(BF16)
    qc_ref[...] = (proj[:, _OFF_QC:_OFF_KR] * (HEAD_DIM ** -0.5)).astype(BF16)

    c, sa, sb = c_ref[...], sa_ref[...], sb_ref[...]
    cqn = _rms_norm(proj[:, _OFF_CQ:_OFF_CKV], qg_ref[...]).astype(BF16)
    qb = _dot(cqn, w_uq_ref[...])
    ckvn = _rms_norm(proj[:, _OFF_CKV:_OFF_QC], kvg_ref[...]).astype(BF16)
    kvb = _dot(ckvn, w_ukv_ref[...])
    kr = _rope_tile(proj[:, _OFF_KR:D_IN_PAD], c, sa, sb).astype(BF16)
    scale = (QK_NOPE + QK_ROPE) ** -0.5 * math.log2(math.e)
    for h in range(B_HEADS):
        lo = h * QK_PAD
        qm_ref[:, lo:lo + QK_NOPE] = (qb[:, lo:lo + QK_NOPE] * scale).astype(BF16)
        qm_ref[:, lo + QK_NOPE:lo + QK_PAD] = (
            _rope_tile(qb[:, lo + QK_NOPE:lo + QK_PAD], c, sa, sb) * scale).astype(BF16)
        km_ref[:, lo:lo + QK_NOPE] = kvb[:, h * QK_NOPE:(h + 1) * QK_NOPE].astype(BF16)
        km_ref[:, lo + QK_NOPE:lo + QK_PAD] = kr
    ones = jnp.ones((V_ROWS - V_DIM, kvb.shape[0]), BF16)
    for h in range(B_HEADS):
        lo = (B_HEADS + h) * QK_NOPE
        vm_ref[h, :V_DIM] = kvb[:, lo:lo + V_DIM].T.astype(BF16)
        vm_ref[h, V_DIM:] = ones


def _project(x2, w_in, q_norm_g, w_uq, kv_norm_g, w_ukv, rope_c, rope_sa, rope_sb, seq, tm=256):
    t = x2.shape[0]
    n_pos = seq // tm
    row = lambda i: (i, 0)
    fixed = lambda i: (0, 0)
    pos = lambda i: (i % n_pos, 0)
    widths = (D_MODEL, A_HEADS * HEAD_DIM, A_KV_HEADS * HEAD_DIM, A_KV_HEADS * HEAD_DIM,
              C_HEADS * HEAD_DIM, B_HEADS * QK_PAD, B_HEADS * QK_PAD)
    out_specs = [pl.BlockSpec((tm, w), row) for w in widths]
    out_shape = [jax.ShapeDtypeStruct((t, w), BF16) for w in widths]
    out_specs.append(pl.BlockSpec((B_HEADS, V_ROWS, tm), lambda i: (0, 0, i)))
    out_shape.append(jax.ShapeDtypeStruct((B_HEADS, V_ROWS, t), BF16))
    return pl.pallas_call(
        _proj_kernel,
        grid=(t // tm,),
        in_specs=[
            pl.BlockSpec((tm, D_MODEL), row),
            _resident((D_MODEL, D_IN_PAD), fixed),
            _resident((1, Q_LORA), fixed),
            _resident((Q_LORA, B_HEADS * QK_PAD), fixed),
            _resident((1, KV_LORA), fixed),
            _resident((KV_LORA, B_HEADS * (QK_NOPE + V_DIM)), fixed),
            pl.BlockSpec((tm, 128), pos),
            pl.BlockSpec((tm, 128), pos),
            pl.BlockSpec((tm, 128), pos),
        ],
        out_specs=out_specs,
        out_shape=out_shape,
        compiler_params=_params(1),
        name="proj",
    )(x2, w_in, q_norm_g, w_uq, kv_norm_g, w_ukv, rope_c, rope_sa, rope_sb)


def _bias_kernel(rb_ref, bucket_ref, band_ref, o_ref):
    bucket = bucket_ref[...]
    band = band_ref[...] != 0
    for h in range(A_HEADS):
        acc = jnp.zeros(bucket.shape, F32)
        for b in range(N_BUCKETS):
            acc = jnp.where(bucket == b, rb_ref[b, h], acc)
        o_ref[h] = jnp.where(band, acc, NEG)


def _t5_bucket(rel):
    half = N_BUCKETS // 2
    max_exact = half // 2
    ret = (rel > 0).astype(jnp.int32) * half
    n = jnp.abs(rel)
    large = max_exact + (jnp.log(jnp.maximum(n, 1).astype(jnp.float32) / max_exact)
                         / math.log(MAX_DISTANCE / max_exact) * (half - max_exact)).astype(jnp.int32)
    large = jnp.minimum(large, half - 1)
    return ret + jnp.where(n < max_exact, n, large)


def _window_bias(rel_bias):
    rel = (jnp.arange(3 * BLOCK) - BLOCK)[None, :] - jnp.arange(BLOCK)[:, None]
    bucket = _t5_bucket(rel).astype(jnp.int32)
    band = (jnp.abs(rel) <= WINDOW).astype(jnp.int32)
    return pl.pallas_call(
        _bias_kernel,
        in_specs=[pl.BlockSpec(memory_space=pltpu.SMEM),
                  pl.BlockSpec(memory_space=pltpu.VMEM),
                  pl.BlockSpec(memory_space=pltpu.VMEM)],
        out_specs=pl.BlockSpec(memory_space=pltpu.VMEM),
        out_shape=jax.ShapeDtypeStruct((A_HEADS, BLOCK, 3 * BLOCK), F32),
        name="win_bias",
    )(rel_bias, bucket, band)


def _win_kernel(sink_ref, q_ref, kp_ref, kc_ref, kn_ref, vp_ref, vc_ref, vn_ref, bias_ref, o_ref,
                kbuf, vbuf, *, tq, seq):
    i = pl.program_id(1)
    kbuf[0:BLOCK] = kp_ref[...]
    kbuf[BLOCK:BLOCK + tq] = kc_ref[...]
    kbuf[BLOCK + tq:] = kn_ref[...]
    vbuf[0:BLOCK] = vp_ref[...]
    vbuf[BLOCK:BLOCK + tq] = vc_ref[...]
    vbuf[BLOCK + tq:] = vn_ref[...]
    rows = A_GROUP * BLOCK
    col = lax.broadcasted_iota(jnp.int32, (rows, 3 * BLOCK), 1)
    row_head = lax.broadcasted_iota(jnp.int32, (rows, 1), 0) // BLOCK
    for g in range(A_KV_HEADS):
        sink = jnp.zeros((rows, 1), F32)
        for r in range(A_GROUP):
            sink = jnp.where(row_head == r, sink_ref[g * A_GROUP + r], sink)
        bias = bias_ref[g * A_GROUP:(g + 1) * A_GROUP].reshape(rows, 3 * BLOCK)
        for j in range(tq // BLOCK):
            kpos = col + (i * tq + (j - 1) * BLOCK)
            valid = (kpos >= 0) & (kpos < seq)
            kw = kbuf[j * BLOCK:(j + 3) * BLOCK, g * HEAD_DIM:(g + 1) * HEAD_DIM]
            vw = vbuf[j * BLOCK:(j + 3) * BLOCK, g * HEAD_DIM:(g + 1) * HEAD_DIM]
            q = jnp.concatenate(
                [q_ref[j * BLOCK:(j + 1) * BLOCK, (g * A_GROUP + r) * HEAD_DIM:(g * A_GROUP + r + 1) * HEAD_DIM]
                 for r in range(A_GROUP)], axis=0)
            s = jnp.where(valid, _dot_nt(q, kw) + bias, NEG)
            m = jnp.maximum(jnp.max(s, axis=-1, keepdims=True), sink)
            p = jnp.exp(s - m)
            den = jnp.sum(p, axis=-1, keepdims=True) + jnp.exp(sink - m)
            o = _dot(p.astype(BF16), vw) / den
            for r in range(A_GROUP):
                h = g * A_GROUP + r
                o_ref[j * BLOCK:(j + 1) * BLOCK, h * HEAD_DIM:(h + 1) * HEAD_DIM] = (
                    o[r * BLOCK:(r + 1) * BLOCK].astype(BF16))


def _window_attention(qa, ka, va, bias, sink, tq=512):
    b, seq, _ = qa.shape
    nb = seq // BLOCK
    per = tq // BLOCK
    kvw = A_KV_HEADS * HEAD_DIM
    prev = lambda bi, i: (bi, jnp.maximum(i * per - 1, 0), 0)
    cur = lambda bi, i: (bi, i, 0)
    nxt = lambda bi, i: (bi, jnp.minimum((i + 1) * per, nb - 1), 0)
    return pl.pallas_call(
        functools.partial(_win_kernel, tq=tq, seq=seq),
        grid=(b, seq // tq),
        in_specs=[
            pl.BlockSpec(memory_space=pltpu.SMEM),
            pl.BlockSpec((None, tq, A_HEADS * HEAD_DIM), cur),
            pl.BlockSpec((None, BLOCK, kvw), prev),
            pl.BlockSpec((None, tq, kvw), cur),
            pl.BlockSpec((None, BLOCK, kvw), nxt),
            pl.BlockSpec((None, BLOCK, kvw), prev),
            pl.BlockSpec((None, tq, kvw), cur),
            pl.BlockSpec((None, BLOCK, kvw), nxt),
            _resident((A_HEADS, BLOCK, 3 * BLOCK), lambda bi, i: (0, 0, 0)),
        ],
        out_specs=pl.BlockSpec((None, tq, A_HEADS * HEAD_DIM), cur),
        out_shape=jax.ShapeDtypeStruct((b, seq, A_HEADS * HEAD_DIM), BF16),
        scratch_shapes=[pltpu.VMEM((tq + 2 * BLOCK, kvw), BF16),
                        pltpu.VMEM((tq + 2 * BLOCK, kvw), BF16)],
        compiler_params=_params(2),
        name="window_attn",
    )(sink, qa, ka, ka, ka, va, va, va, bias)


def _mla_kernel(q_ref, k_ref, vt_ref, o_ref, s_buf, p_buf, qt_ref, m_ref, *, tk, nk, nq):
    i = pl.program_id(2)
    tq = q_ref.shape[0]
    grp = tk // 8
    neg_inf = jnp.full((8, tq), -jnp.inf, F32)

    def transpose_q():
        qt_ref[...] = q_ref[...].T

    def new_scores(c):
        st = _dot(k_ref[c * tk:(c + 1) * tk, :], qt_ref[...])
        s_buf[c * tk:(c + 1) * tk, :] = st
        return jnp.max(st.reshape(grp, 8, tq), axis=0)

    def exp_scores(c, m8):
        p = jnp.exp2(s_buf[c * tk:(c + 1) * tk, :].reshape(grp, 8, tq) - m8[None])
        p_buf[c * tk:(c + 1) * tk, :] = p.reshape(tk, tq).astype(BF16)

    def set_max(mx):
        m_ref[...] = jnp.broadcast_to(jnp.max(mx, axis=0, keepdims=True), (8, tq))

    @pl.when(i == 0)
    def _():
        transpose_q()
        mx = neg_inf
        for c in range(nk):
            mx = jnp.maximum(mx, new_scores(c))
        set_max(mx)

    @pl.when((i > 0) & (i < nq))
    def _():
        transpose_q()
        m8 = m_ref[...]
        mx = neg_inf
        exp_scores(0, m8)
        for c in range(1, nk):
            exp_scores(c, m8)
            mx = jnp.maximum(mx, new_scores(c - 1))
        set_max(jnp.maximum(mx, new_scores(nk - 1)))

    @pl.when(i == nq)
    def _():
        m8 = m_ref[...]
        for c in range(nk):
            exp_scores(c, m8)

    @pl.when(i > 0)
    def _():
        pv = _dot(vt_ref[...], p_buf[...])
        o_t = pv[:V_DIM] / pv[V_DIM:V_DIM + 1]
        o_ref[...] = o_t.T.astype(o_ref.dtype)


def _mla_attention(qm, km, vt, tq=256, tk=1024):
    b, seq, _ = qm.shape
    nq = seq // tq
    assert seq % tk == 0 and seq % tq == 0
    return pl.pallas_call(
        functools.partial(_mla_kernel, tk=tk, nk=seq // tk, nq=nq),
        grid=(b, B_HEADS, nq + 1),
        in_specs=[
            pl.BlockSpec((None, tq, QK_PAD), lambda bi, h, i: (bi, jnp.minimum(i, nq - 1), h)),
            pl.BlockSpec((None, seq, QK_PAD), lambda bi, h, i: (bi, 0, h), pipeline_mode=pl.Buffered(1)),
            pl.BlockSpec((None, V_ROWS, seq), lambda bi, h, i: (h, 0, bi), pipeline_mode=pl.Buffered(1)),
        ],
        out_specs=pl.BlockSpec((None, tq, V_DIM), lambda bi, h, i: (bi, jnp.maximum(i - 1, 0), h)),
        out_shape=jax.ShapeDtypeStruct((b, seq, B_HEADS * V_DIM), BF16),
        scratch_shapes=[pltpu.VMEM((seq, tq), F32), pltpu.VMEM((seq, tq), BF16), pltpu.VMEM((QK_PAD, tq), BF16),
                        pltpu.VMEM((8, tq), F32)],
        compiler_params=_params(3),
        name="mla_attn",
    )(qm, km, vt)


def _memkv_kernel(m_ref, w_ref, o_ref):
    o_ref[...] = _dot(m_ref[...].astype(BF16), w_ref[...]).astype(BF16)


def _memory_kv(mem2, w_mem_kv):
    t = mem2.shape[0]
    n = w_mem_kv.shape[1]
    return pl.pallas_call(
        _memkv_kernel,
        grid=(t // N_MEM,),
        in_specs=[pl.BlockSpec((N_MEM, D_MODEL), lambda i: (i, 0)),
                  _resident((D_MODEL, n), lambda i: (0, 0))],
        out_specs=pl.BlockSpec((N_MEM, n), lambda i: (i, 0)),
        out_shape=jax.ShapeDtypeStruct((t, n), BF16),
        compiler_params=_params(1),
        name="mem_kv",
    )(mem2, w_mem_kv)


def _cross_kernel(q_ref, k_ref, v_ref, o_ref):
    for h in range(C_HEADS):
        sl = slice(h * HEAD_DIM, (h + 1) * HEAD_DIM)
        s = _dot_nt(q_ref[:, sl], k_ref[:, sl])
        m = jnp.max(s, axis=-1, keepdims=True)
        p = jnp.exp(s - m)
        den = jnp.sum(p, axis=-1, keepdims=True)
        o_ref[:, sl] = (_dot(p.astype(BF16), v_ref[:, sl]) / den).astype(BF16)


def _cross_attention(qc, mkv, tq=512):
    b, seq, w = qc.shape
    return pl.pallas_call(
        _cross_kernel,
        grid=(b, seq // tq),
        in_specs=[pl.BlockSpec((None, tq, w), lambda bi, i: (bi, i, 0)),
                  pl.BlockSpec((None, N_MEM, w), lambda bi, i: (bi, 0, 0)),
                  pl.BlockSpec((None, N_MEM, w), lambda bi, i: (bi, 0, 1))],
        out_specs=pl.BlockSpec((None, tq, w), lambda bi, i: (bi, i, 0)),
        out_shape=jax.ShapeDtypeStruct((b, seq, w), BF16),
        compiler_params=_params(2),
        name="cross_attn",
    )(qc, mkv, mkv)


def _merge_kernel(xb_ref, a_ref, b_ref, c_ref, wg0_ref, wg1_ref, wg2_ref, bg0_ref, bg1_ref, bg2_ref,
                  wa_ref, wb_ref, wc_ref, o_ref):
    xb = xb_ref[...]
    acc = jax.nn.sigmoid(_dot(xb, wg0_ref[...]) + bg0_ref[...]) * _dot(a_ref[...], wa_ref[...])
    acc += jax.nn.sigmoid(_dot(xb, wg1_ref[...]) + bg1_ref[...]) * _dot(b_ref[...], wb_ref[...])
    acc += jax.nn.sigmoid(_dot(xb, wg2_ref[...]) + bg2_ref[...]) * _dot(c_ref[...], wc_ref[...])
    o_ref[...] = acc.astype(BF16)


def _gated_merge(xb, a, b, c, w_gate, b_gate, w_br_a, w_br_b, w_br_c, tm=1024, tn=512):
    t = xb.shape[0]
    nj = D_MODEL // tn
    row = lambda i, j: (i, 0)
    colj = lambda i, j: (0, j)
    gate = lambda n: (lambda i, j: (0, n * nj + j))
    return pl.pallas_call(
        _merge_kernel,
        grid=(t // tm, nj),
        in_specs=[
            pl.BlockSpec((tm, D_MODEL), row),
            pl.BlockSpec((tm, a.shape[1]), row),
            pl.BlockSpec((tm, b.shape[1]), row),
            pl.BlockSpec((tm, c.shape[1]), row),
            pl.BlockSpec((D_MODEL, tn), gate(0)),
            pl.BlockSpec((D_MODEL, tn), gate(1)),
            pl.BlockSpec((D_MODEL, tn), gate(2)),
            pl.BlockSpec((1, tn), gate(0)),
            pl.BlockSpec((1, tn), gate(1)),
            pl.BlockSpec((1, tn), gate(2)),
            pl.BlockSpec((a.shape[1], tn), colj),
            pl.BlockSpec((b.shape[1], tn), colj),
            pl.BlockSpec((c.shape[1], tn), colj),
        ],
        out_specs=pl.BlockSpec((tm, tn), lambda i, j: (i, j)),
        out_shape=jax.ShapeDtypeStruct((t, D_MODEL), BF16),
        compiler_params=_params(2),
        name="gated_merge",
    )(xb, a, b, c, w_gate, w_gate, w_gate, b_gate, b_gate, b_gate, w_br_a, w_br_b, w_br_c)


def _resid_ln_kernel(u_ref, w_ref, r_ref, g_ref, b_ref, *o_refs):
    y = _layer_norm(ALPHA * r_ref[...] + _dot(u_ref[...], w_ref[...]), g_ref[...], b_ref[...])
    o_refs[0][...] = y
    if len(o_refs) > 1:
        o_refs[1][...] = y.astype(BF16)


def _proj_resid_ln(u, w, resid, g, b, tm, with_bf16):
    t, k = u.shape
    row = lambda i: (i, 0)
    fixed = lambda i: (0, 0)
    out_shape = [jax.ShapeDtypeStruct((t, D_MODEL), F32)]
    if with_bf16:
        out_shape.append(jax.ShapeDtypeStruct((t, D_MODEL), BF16))
    return pl.pallas_call(
        _resid_ln_kernel,
        grid=(t // tm,),
        in_specs=[pl.BlockSpec((tm, k), row),
                  _resident((k, D_MODEL), fixed),
                  pl.BlockSpec((tm, D_MODEL), row),
                  _resident((1, D_MODEL), fixed),
                  _resident((1, D_MODEL), fixed)],
        out_specs=[pl.BlockSpec((tm, D_MODEL), row) for _ in out_shape],
        out_shape=out_shape,
        compiler_params=_params(1),
        name="proj_resid_ln",
    )(u, w, resid, g, b)


def _swiglu_kernel(h_ref, wg_ref, wu_ref, o_ref):
    h = h_ref[...]
    gate = _dot(h, wg_ref[...])
    up = _dot(h, wu_ref[...])
    o_ref[...] = (gate * jax.nn.sigmoid(gate) * up).astype(BF16)


def _swiglu_in(hb, w_ffn_in, tm=1024, tn=512):
    t = hb.shape[0]
    nj = D_FF // tn
    return pl.pallas_call(
        _swiglu_kernel,
        grid=(t // tm, nj),
        in_specs=[pl.BlockSpec((tm, D_MODEL), lambda i, j: (i, 0)),
                  pl.BlockSpec((D_MODEL, tn), lambda i, j: (0, j)),
                  pl.BlockSpec((D_MODEL, tn), lambda i, j: (0, nj + j))],
        out_specs=pl.BlockSpec((tm, tn), lambda i, j: (i, j)),
        out_shape=jax.ShapeDtypeStruct((t, D_FF), BF16),
        compiler_params=_params(2),
        name="swiglu_in",
    )(hb, w_ffn_in, w_ffn_in)


def _rope_tables(seq):
    inv = 1.0 / (ROPE_THETA ** (jnp.arange(ROPE_HALF, dtype=jnp.float32) / ROPE_HALF))
    ang = jnp.arange(seq, dtype=jnp.float32)[:, None] * inv[None, :]
    cos, sin = jnp.cos(ang), jnp.sin(ang)
    z = jnp.zeros_like(cos)
    return (jnp.concatenate([cos, cos, z, z], axis=-1),
            jnp.concatenate([-sin, z, z, z], axis=-1),
            jnp.concatenate([z, sin, z, z], axis=-1))


def _prepare_weights(w_in, w_uq, w_ukv):
    parts = np.cumsum((A_HEADS * HEAD_DIM, A_KV_HEADS * HEAD_DIM, A_KV_HEADS * HEAD_DIM, Q_LORA, KV_LORA, QK_ROPE))
    qa, ka, va, cq, ckv, kr, qc = jnp.split(w_in, [int(p) for p in parts], axis=-1)
    pad = jnp.zeros((D_MODEL, 128 - QK_ROPE), w_in.dtype)
    w_in_p = jnp.concatenate([qa, ka, va, cq, ckv, qc, kr, pad], axis=-1).astype(BF16)
    w_uq_p = jnp.pad(w_uq.reshape(Q_LORA, B_HEADS, QK_NOPE + QK_ROPE),
                     ((0, 0), (0, 0), (0, QK_PAD - QK_NOPE - QK_ROPE))).reshape(Q_LORA, B_HEADS * QK_PAD).astype(BF16)
    kv = w_ukv.reshape(KV_LORA, B_HEADS, 2, QK_NOPE)
    w_ukv_p = jnp.concatenate([kv[:, :, 0].reshape(KV_LORA, -1), kv[:, :, 1].reshape(KV_LORA, -1)],
                              axis=-1).astype(BF16)
    return w_in_p, w_uq_p, w_ukv_p


def _encoder_layer(x, mem, bias, w):
    b, seq, d = x.shape
    t = b * seq
    x2 = x.reshape(t, d)
    rope_c, rope_sa, rope_sb = _rope_tables(seq)
    xb, qa, ka, va, qc, qm, km, vt = _project(
        x2, w["w_in"], w["q_norm_g"], w["w_uq"], w["kv_norm_g"], w["w_ukv"], rope_c, rope_sa, rope_sb, seq)
    r3 = lambda v: v.reshape(b, seq, v.shape[-1])
    a_out = _window_attention(r3(qa), r3(ka), r3(va), bias, w["sink"])
    b_out = _mla_attention(r3(qm), r3(km), vt)
    mkv = _memory_kv(mem.reshape(b * N_MEM, d), w["w_mem_kv"]).reshape(b, N_MEM, -1)
    c_out = _cross_attention(r3(qc), mkv)
    merged = _gated_merge(xb, a_out.reshape(t, -1), b_out.reshape(t, -1), c_out.reshape(t, -1),
                          w["w_gate"], w["b_gate"], w["w_br_a"], w["w_br_b"], w["w_br_c"])
    h, hb = _proj_resid_ln(merged, w["w_o"], x2, w["ln1_g"], w["ln1_b"], tm=512, with_bf16=True)
    act = _swiglu_in(hb, w["w_ffn_in"])
    (y,) = _proj_resid_ln(act, w["w_ffn_down"], h, w["ln2_g"], w["ln2_b"], tm=256, with_bf16=False)
    return y.reshape(b, seq, d)


def kernel(x_prompt, x_sample, mem_prompt, mem_sample, w_in, rel_bias, sink, q_norm_g, w_uq, kv_norm_g, w_ukv,
           w_mem_kv, w_gate, b_gate, w_br_a, w_br_b, w_br_c, w_o, ln1_g, ln1_b, w_ffn_in, w_ffn_down, ln2_g, ln2_b):
    bias = _window_bias(rel_bias)
    xp, xs = x_prompt, x_sample
    for l in range(DEPTH):
        w_in_p, w_uq_p, w_ukv_p = _prepare_weights(w_in[l], w_uq[l], w_ukv[l])
        w = dict(
            w_in=w_in_p, w_uq=w_uq_p, w_ukv=w_ukv_p, sink=sink[l],
            q_norm_g=q_norm_g[l][None, :], kv_norm_g=kv_norm_g[l][None, :],
            w_mem_kv=w_mem_kv[l].astype(BF16), w_gate=w_gate[l].astype(BF16), b_gate=b_gate[l][None, :],
            w_br_a=w_br_a[l].astype(BF16), w_br_b=w_br_b[l].astype(BF16), w_br_c=w_br_c[l].astype(BF16),
            w_o=w_o[l].astype(BF16), ln1_g=ln1_g[l][None, :], ln1_b=ln1_b[l][None, :],
            w_ffn_in=w_ffn_in[l].astype(BF16), w_ffn_down=w_ffn_down[l].astype(BF16),
            ln2_g=ln2_g[l][None, :], ln2_b=ln2_b[l][None, :])
        xp = _encoder_layer(xp, mem_prompt, bias, w)
        xs = _encoder_layer(xs, mem_sample, bias, w)
    return (xp, xs)
```

```python
import functools
import math

import jax
import jax.numpy as jnp
import numpy as np
from jax import lax
from jax.experimental import pallas as pl
from jax.experimental.pallas import tpu as pltpu

F32 = jnp.float32
BF16 = jnp.bfloat16

D_MODEL = 2048
HEAD_DIM = 128
A_HEADS = 6
A_KV_HEADS = 2
A_GROUP = A_HEADS // A_KV_HEADS
WINDOW = 128
BLOCK = 128
N_BUCKETS = 32
MAX_DISTANCE = 128
B_HEADS = 6
Q_LORA = 512
KV_LORA = 512
QK_NOPE = 128
QK_ROPE = 64
V_DIM = 128
ROPE_THETA = 10000.0
C_HEADS = 4
N_MEM = 256
N_BRANCH = 3
D_FF = -(-8 * D_MODEL // (3 * 256)) * 256
DEPTH = 1
ALPHA = (2 * DEPTH) ** 0.25
LN_EPS = 1e-5
RMS_EPS = 1e-6
NEG = -1e30

QK_PAD = 256
ROPE_HALF = QK_ROPE // 2
V_ROWS = V_DIM + 16
VMEM_LIMIT = 56 * 1024 * 1024

_OFF_QA = 0
_OFF_KA = _OFF_QA + A_HEADS * HEAD_DIM
_OFF_VA = _OFF_KA + A_KV_HEADS * HEAD_DIM
_OFF_CQ = _OFF_VA + A_KV_HEADS * HEAD_DIM
_OFF_CKV = _OFF_CQ + Q_LORA
_OFF_QC = _OFF_CKV + KV_LORA
_OFF_KR = _OFF_QC + C_HEADS * HEAD_DIM
D_IN_PAD = _OFF_KR + 128


def _params(n_axes):
    return pltpu.CompilerParams(dimension_semantics=("arbitrary",) * n_axes,
                                vmem_limit_bytes=VMEM_LIMIT)


def _resident(shape, index_map):
    return pl.BlockSpec(shape, index_map, pipeline_mode=pl.Buffered(1))


def _dot(a, b):
    return jnp.dot(a, b, preferred_element_type=F32)


def _dot_nt(a, b):
    return lax.dot_general(a, b, (((1,), (1,)), ((), ())), preferred_element_type=F32)


def _layer_norm(v, g, b):
    mu = jnp.mean(v, axis=-1, keepdims=True)
    c = v - mu
    var = jnp.mean(c * c, axis=-1, keepdims=True)
    return c * lax.rsqrt(var + LN_EPS) * g + b


def _rms_norm(v, g):
    ms = jnp.mean(v * v, axis=-1, keepdims=True)
    return v * lax.rsqrt(ms + RMS_EPS) * g


def _rope_tile(t, c, sa, sb):
    return t * c + pltpu.roll(t, 128 - ROPE_HALF, 1) * sa + pltpu.roll(t, ROPE_HALF, 1) * sb


def _proj_kernel(x_ref, w_in_ref, qg_ref, w_uq_ref, kvg_ref, w_ukv_ref, c_ref, sa_ref, sb_ref,
                 xb_ref, qa_ref, ka_ref, va_ref, qc_ref, qm_ref, km_ref, vm_ref):
    xb = x_ref[...].astype(BF16)
    xb_ref[...] = xb
    proj = _dot(xb, w_in_ref[...])
    qa_ref[...] = (proj[:, _OFF_QA:_OFF_KA] * (HEAD_DIM ** -0.5)).astype(BF16)
    ka_ref[...] = proj[:, _OFF_KA:_OFF_VA].astype(BF16)
    va_ref[...] = proj[:, _OFF_VA:_OFF_CQ].astype(BF16)
    qc_ref[...] = (proj[:, _OFF_QC:_OFF_KR] * (HEAD_DIM ** -0.5)).astype(BF16)

    c, sa, sb = c_ref[...], sa_ref[...], sb_ref[...]
    cqn = _rms_norm(proj[:, _OFF_CQ:_OFF_CKV], qg_ref[...]).astype(BF16)
    qb = _dot(cqn, w_uq_ref[...])
    ckvn = _rms_norm(proj[:, _OFF_CKV:_OFF_QC], kvg_ref[...]).astype(BF16)
    kvb = _dot(ckvn, w_ukv_ref[...])
    kr = _rope_tile(proj[:, _OFF_KR:D_IN_PAD], c, sa, sb).astype(BF16)
    scale = (QK_NOPE + QK_ROPE) ** -0.5 * math.log2(math.e)
    for h in range(B_HEADS):
        lo = h * QK_PAD
        qm_ref[:, lo:lo + QK_NOPE] = (qb[:, lo:lo + QK_NOPE] * scale).astype(BF16)
        qm_ref[:, lo + QK_NOPE:lo + QK_PAD] = (
            _rope_tile(qb[:, lo + QK_NOPE:lo + QK_PAD], c, sa, sb) * scale).astype(BF16)
        km_ref[:, lo:lo + QK_NOPE] = kvb[:, h * QK_NOPE:(h + 1) * QK_NOPE].astype(BF16)
        km_ref[:, lo + QK_NOPE:lo + QK_PAD] = kr
    ones = jnp.ones((V_ROWS - V_DIM, kvb.shape[0]), BF16)
    for h in range(B_HEADS):
        lo = (B_HEADS + h) * QK_NOPE
        vm_ref[h, :V_DIM] = kvb[:, lo:lo + V_DIM].T.astype(BF16)
        vm_ref[h, V_DIM:] = ones


def _project(x2, w_in, q_norm_g, w_uq, kv_norm_g, w_ukv, rope_c, rope_sa, rope_sb, seq, tm=256):
    t = x2.shape[0]
    n_pos = seq // tm
    row = lambda i: (i, 0)
    fixed = lambda i: (0, 0)
    pos = lambda i: (i % n_pos, 0)
    widths = (D_MODEL, A_HEADS * HEAD_DIM, A_KV_HEADS * HEAD_DIM, A_KV_HEADS * HEAD_DIM,
              C_HEADS * HEAD_DIM, B_HEADS * QK_PAD, B_HEADS * QK_PAD)
    out_specs = [pl.BlockSpec((tm, w), row) for w in widths]
    out_shape = [jax.ShapeDtypeStruct((t, w), BF16) for w in widths]
    out_specs.append(pl.BlockSpec((B_HEADS, V_ROWS, tm), lambda i: (0, 0, i)))
    out_shape.append(jax.ShapeDtypeStruct((B_HEADS, V_ROWS, t), BF16))
    return pl.pallas_call(
        _proj_kernel,
        grid=(t // tm,),
        in_specs=[
            pl.BlockSpec((tm, D_MODEL), row),
            _resident((D_MODEL, D_IN_PAD), fixed),
            _resident((1, Q_LORA), fixed),
            _resident((Q_LORA, B_HEADS * QK_PAD), fixed),
            _resident((1, KV_LORA), fixed),
            _resident((KV_LORA, B_HEADS * (QK_NOPE + V_DIM)), fixed),
            pl.BlockSpec((tm, 128), pos),
            pl.BlockSpec((tm, 128), pos),
            pl.BlockSpec((tm, 128), pos),
        ],
        out_specs=out_specs,
        out_shape=out_shape,
        compiler_params=_params(1),
        name="proj",
    )(x2, w_in, q_norm_g, w_uq, kv_norm_g, w_ukv, rope_c, rope_sa, rope_sb)


def _bias_kernel(rb_ref, bucket_ref, band_ref, o_ref):
    bucket = bucket_ref[...]
    band = band_ref[...] != 0
    for h in range(A_HEADS):
        acc = jnp.zeros(bucket.shape, F32)
        for b in range(N_BUCKETS):
            acc = jnp.where(bucket == b, rb_ref[b, h], acc)
        o_ref[h] = jnp.where(band, acc, NEG)


def _t5_bucket(rel):
    half = N_BUCKETS // 2
    max_exact = half // 2
    ret = (rel > 0).astype(jnp.int32) * half
    n = jnp.abs(rel)
    large = max_exact + (jnp.log(jnp.maximum(n, 1).astype(jnp.float32) / max_exact)
                         / math.log(MAX_DISTANCE / max_exact) * (half - max_exact)).astype(jnp.int32)
    large = jnp.minimum(large, half - 1)
    return ret + jnp.where(n < max_exact, n, large)


def _window_bias(rel_bias):
    rel = (jnp.arange(3 * BLOCK) - BLOCK)[None, :] - jnp.arange(BLOCK)[:, None]
    bucket = _t5_bucket(rel).astype(jnp.int32)
    band = (jnp.abs(rel) <= WINDOW).astype(jnp.int32)
    return pl.pallas_call(
        _bias_kernel,
        in_specs=[pl.BlockSpec(memory_space=pltpu.SMEM),
                  pl.BlockSpec(memory_space=pltpu.VMEM),
                  pl.BlockSpec(memory_space=pltpu.VMEM)],
        out_specs=pl.BlockSpec(memory_space=pltpu.VMEM),
        out_shape=jax.ShapeDtypeStruct((A_HEADS, BLOCK, 3 * BLOCK), F32),
        name="win_bias",
    )(rel_bias, bucket, band)


def _win_kernel(sink_ref, q_ref, kp_ref, kc_ref, kn_ref, vp_ref, vc_ref, vn_ref, bias_ref, o_ref,
                kbuf, vbuf, *, tq, seq):
    i = pl.program_id(1)
    kbuf[0:BLOCK] = kp_ref[...]
    kbuf[BLOCK:BLOCK + tq] = kc_ref[...]
    kbuf[BLOCK + tq:] = kn_ref[...]
    vbuf[0:BLOCK] = vp_ref[...]
    vbuf[BLOCK:BLOCK + tq] = vc_ref[...]
    vbuf[BLOCK + tq:] = vn_ref[...]
    rows = A_GROUP * BLOCK
    n_sub = tq // BLOCK
    n_tiles = seq // tq
    col = lax.broadcasted_iota(jnp.int32, (rows, 3 * BLOCK), 1)
    row_head = lax.broadcasted_iota(jnp.int32, (rows, 1), 0) // BLOCK
    for g in range(A_KV_HEADS):
        sink = jnp.zeros((rows, 1), F32)
        for r in range(A_GROUP):
            sink = jnp.where(row_head == r, sink_ref[g * A_GROUP + r], sink)
        bias = bias_ref[g * A_GROUP:(g + 1) * A_GROUP].reshape(rows, 3 * BLOCK)
        for j in range(n_sub):
            kw = kbuf[j * BLOCK:(j + 3) * BLOCK, g * HEAD_DIM:(g + 1) * HEAD_DIM]
            vw = vbuf[j * BLOCK:(j + 3) * BLOCK, g * HEAD_DIM:(g + 1) * HEAD_DIM]
            q = jnp.concatenate(
                [q_ref[j * BLOCK:(j + 1) * BLOCK, (g * A_GROUP + r) * HEAD_DIM:(g * A_GROUP + r + 1) * HEAD_DIM]
                 for r in range(A_GROUP)], axis=0)
            s = _dot_nt(q, kw) + bias
            if j == 0:
                s = jnp.where(jnp.logical_and(i == 0, col < BLOCK), NEG, s)
            if j == n_sub - 1:
                s = jnp.where(jnp.logical_and(i == n_tiles - 1, col >= 2 * BLOCK), NEG, s)
            m = jnp.maximum(jnp.max(s, axis=-1, keepdims=True), sink)
            p = jnp.exp(s - m)
            den = jnp.sum(p, axis=-1, keepdims=True) + jnp.exp(sink - m)
            o = _dot(p.astype(BF16), vw) / den
            for r in range(A_GROUP):
                h = g * A_GROUP + r
                o_ref[j * BLOCK:(j + 1) * BLOCK, h * HEAD_DIM:(h + 1) * HEAD_DIM] = (
                    o[r * BLOCK:(r + 1) * BLOCK].astype(BF16))


def _window_attention(qa, ka, va, bias, sink, tq=512):
    b, seq, _ = qa.shape
    nb = seq // BLOCK
    per = tq // BLOCK
    kvw = A_KV_HEADS * HEAD_DIM
    prev = lambda bi, i: (bi, jnp.maximum(i * per - 1, 0), 0)
    cur = lambda bi, i: (bi, i, 0)
    nxt = lambda bi, i: (bi, jnp.minimum((i + 1) * per, nb - 1), 0)
    return pl.pallas_call(
        functools.partial(_win_kernel, tq=tq, seq=seq),
        grid=(b, seq // tq),
        in_specs=[
            pl.BlockSpec(memory_space=pltpu.SMEM),
            pl.BlockSpec((None, tq, A_HEADS * HEAD_DIM), cur),
            pl.BlockSpec((None, BLOCK, kvw), prev),
            pl.BlockSpec((None, tq, kvw), cur),
            pl.BlockSpec((None, BLOCK, kvw), nxt),
            pl.BlockSpec((None, BLOCK, kvw), prev),
            pl.BlockSpec((None, tq, kvw), cur),
            pl.BlockSpec((None, BLOCK, kvw), nxt),
            _resident((A_HEADS, BLOCK, 3 * BLOCK), lambda bi, i: (0, 0, 0)),
        ],
        out_specs=pl.BlockSpec((None, tq, A_HEADS * HEAD_DIM), cur),
        out_shape=jax.ShapeDtypeStruct((b, seq, A_HEADS * HEAD_DIM), BF16),
        scratch_shapes=[pltpu.VMEM((tq + 2 * BLOCK, kvw), BF16),
                        pltpu.VMEM((tq + 2 * BLOCK, kvw), BF16)],
        compiler_params=_params(2),
        name="window_attn",
    )(sink, qa, ka, ka, ka, va, va, va, bias)


def _mla_kernel(q_ref, k_ref, vt_ref, o_ref, s_buf, p_buf, qt_ref, m_ref, *, tk, nk, nq):
    i = pl.program_id(2)
    tq = q_ref.shape[0]
    grp = tk // 8
    neg_inf = jnp.full((8, tq), -jnp.inf, F32)

    def transpose_q():
        qt_ref[...] = q_ref[...].T

    def new_scores(c):
        st = _dot(k_ref[c * tk:(c + 1) * tk, :], qt_ref[...])
        s_buf[c * tk:(c + 1) * tk, :] = st
        return jnp.max(st.reshape(grp, 8, tq), axis=0)

    def exp_scores(c, m8):
        p = jnp.exp2(s_buf[c * tk:(c + 1) * tk, :].reshape(grp, 8, tq) - m8[None])
        p_buf[c * tk:(c + 1) * tk, :] = p.reshape(tk, tq).astype(BF16)

    def set_max(mx):
        m_ref[...] = jnp.broadcast_to(jnp.max(mx, axis=0, keepdims=True), (8, tq))

    @pl.when(i == 0)
    def _():
        transpose_q()
        mx = neg_inf
        for c in range(nk):
            mx = jnp.maximum(mx, new_scores(c))
        set_max(mx)

    @pl.when((i > 0) & (i < nq))
    def _():
        transpose_q()
        m8 = m_ref[...]
        mx = neg_inf
        exp_scores(0, m8)
        for c in range(1, nk):
            exp_scores(c, m8)
            mx = jnp.maximum(mx, new_scores(c - 1))
        set_max(jnp.maximum(mx, new_scores(nk - 1)))

    @pl.when(i == nq)
    def _():
        m8 = m_ref[...]
        for c in range(nk):
            exp_scores(c, m8)

    @pl.when(i > 0)
    def _():
        pv = _dot(vt_ref[...], p_buf[...])
        o_t = pv[:V_DIM] / pv[V_DIM:V_DIM + 1]
        o_ref[...] = o_t.T.astype(o_ref.dtype)


def _mla_attention(qm, km, vt, tq=256, tk=1024):
    b, seq, _ = qm.shape
    nq = seq // tq
    assert seq % tk == 0 and seq % tq == 0
    return pl.pallas_call(
        functools.partial(_mla_kernel, tk=tk, nk=seq // tk, nq=nq),
        grid=(b, B_HEADS, nq + 1),
        in_specs=[
            pl.BlockSpec((None, tq, QK_PAD), lambda bi, h, i: (bi, jnp.minimum(i, nq - 1), h)),
            pl.BlockSpec((None, seq, QK_PAD), lambda bi, h, i: (bi, 0, h), pipeline_mode=pl.Buffered(1)),
            pl.BlockSpec((None, V_ROWS, seq), lambda bi, h, i: (h, 0, bi), pipeline_mode=pl.Buffered(1)),
        ],
        out_specs=pl.BlockSpec((None, tq, V_DIM), lambda bi, h, i: (bi, jnp.maximum(i - 1, 0), h)),
        out_shape=jax.ShapeDtypeStruct((b, seq, B_HEADS * V_DIM), BF16),
        scratch_shapes=[pltpu.VMEM((seq, tq), F32), pltpu.VMEM((seq, tq), BF16), pltpu.VMEM((QK_PAD, tq), BF16),
                        pltpu.VMEM((8, tq), F32)],
        compiler_params=_params(3),
        name="mla_attn",
    )(qm, km, vt)


def _memkv_kernel(m_ref, w_ref, o_ref):
    o_ref[...] = _dot(m_ref[...].astype(BF16), w_ref[...]).astype(BF16)


def _memory_kv(mem2, w_mem_kv):
    t = mem2.shape[0]
    n = w_mem_kv.shape[1]
    return pl.pallas_call(
        _memkv_kernel,
        grid=(t // N_MEM,),
        in_specs=[pl.BlockSpec((N_MEM, D_MODEL), lambda i: (i, 0)),
                  _resident((D_MODEL, n), lambda i: (0, 0))],
        out_specs=pl.BlockSpec((N_MEM, n), lambda i: (i, 0)),
        out_shape=jax.ShapeDtypeStruct((t, n), BF16),
        compiler_params=_params(1),
        name="mem_kv",
    )(mem2, w_mem_kv)


def _cross_kernel(q_ref, k_ref, v_ref, o_ref):
    for h in range(C_HEADS):
        sl = slice(h * HEAD_DIM, (h + 1) * HEAD_DIM)
        s = _dot_nt(q_ref[:, sl], k_ref[:, sl])
        m = jnp.max(s, axis=-1, keepdims=True)
        p = jnp.exp(s - m)
        den = jnp.sum(p, axis=-1, keepdims=True)
        o_ref[:, sl] = (_dot(p.astype(BF16), v_ref[:, sl]) / den).astype(BF16)


def _cross_attention(qc, mkv, tq=512):
    b, seq, w = qc.shape
    return pl.pallas_call(
        _cross_kernel,
        grid=(b, seq // tq),
        in_specs=[pl.BlockSpec((None, tq, w), lambda bi, i: (bi, i, 0)),
                  pl.BlockSpec((None, N_MEM, w), lambda bi, i: (bi, 0, 0)),
                  pl.BlockSpec((None, N_MEM, w), lambda bi, i: (bi, 0, 1))],
        out_specs=pl.BlockSpec((None, tq, w), lambda bi, i: (bi, i, 0)),
        out_shape=jax.ShapeDtypeStruct((b, seq, w), BF16),
        compiler_params=_params(2),
        name="cross_attn",
    )(qc, mkv, mkv)


def _merge_kernel(xb_ref, a_ref, b_ref, c_ref, wg0_ref, wg1_ref, wg2_ref, bg0_ref, bg1_ref, bg2_ref,
                  wa_ref, wb_ref, wc_ref, o_ref):
    xb = xb_ref[...]
    acc = jax.nn.sigmoid(_dot(xb, wg0_ref[...]) + bg0_ref[...]) * _dot(a_ref[...], wa_ref[...])
    acc += jax.nn.sigmoid(_dot(xb, wg1_ref[...]) + bg1_ref[...]) * _dot(b_ref[...], wb_ref[...])
    acc += jax.nn.sigmoid(_dot(xb, wg2_ref[...]) + bg2_ref[...]) * _dot(c_ref[...], wc_ref[...])
    o_ref[...] = acc.astype(BF16)


def _gated_merge(xb, a, b, c, w_gate, b_gate, w_br_a, w_br_b, w_br_c, tm=1024, tn=512):
    t = xb.shape[0]
    nj = D_MODEL // tn
    row = lambda i, j: (i, 0)
    colj = lambda i, j: (0, j)
    gate = lambda n: (lambda i, j: (0, n * nj + j))
    return pl.pallas_call(
        _merge_kernel,
        grid=(t // tm, nj),
        in_specs=[
            pl.BlockSpec((tm, D_MODEL), row),
            pl.BlockSpec((tm, a.shape[1]), row),
            pl.BlockSpec((tm, b.shape[1]), row),
            pl.BlockSpec((tm, c.shape[1]), row),
            pl.BlockSpec((D_MODEL, tn), gate(0)),
            pl.BlockSpec((D_MODEL, tn), gate(1)),
            pl.BlockSpec((D_MODEL, tn), gate(2)),
            pl.BlockSpec((1, tn), gate(0)),
            pl.BlockSpec((1, tn), gate(1)),
            pl.BlockSpec((1, tn), gate(2)),
            pl.BlockSpec((a.shape[1], tn), colj),
            pl.BlockSpec((b.shape[1], tn), colj),
            pl.BlockSpec((c.shape[1], tn), colj),
        ],
        out_specs=pl.BlockSpec((tm, tn), lambda i, j: (i, j)),
        out_shape=jax.ShapeDtypeStruct((t, D_MODEL), BF16),
        compiler_params=_params(2),
        name="gated_merge",
    )(xb, a, b, c, w_gate, w_gate, w_gate, b_gate, b_gate, b_gate, w_br_a, w_br_b, w_br_c)


def _resid_ln_kernel(u_ref, w_ref, r_ref, g_ref, b_ref, *refs, n):
    *o_refs, z_ref = refs
    i = pl.program_id(0)

    def matmul():
        z_ref[...] = ALPHA * r_ref[...] + _dot(u_ref[...], w_ref[...])

    def norm():
        y = _layer_norm(z_ref[...], g_ref[...], b_ref[...])
        o_refs[0][...] = y
        if len(o_refs) > 1:
            o_refs[1][...] = y.astype(BF16)

    @pl.when(i == 0)
    def _():
        matmul()

    @pl.when((i > 0) & (i < n))
    def _():
        norm()
        matmul()

    @pl.when(i == n)
    def _():
        norm()


def _proj_resid_ln(u, w, resid, g, b, tm, with_bf16):
    t, k = u.shape
    n = t // tm
    row_in = lambda i: (jnp.minimum(i, n - 1), 0)
    row_out = lambda i: (jnp.maximum(i - 1, 0), 0)
    fixed = lambda i: (0, 0)
    out_shape = [jax.ShapeDtypeStruct((t, D_MODEL), F32)]
    if with_bf16:
        out_shape.append(jax.ShapeDtypeStruct((t, D_MODEL), BF16))
    return pl.pallas_call(
        functools.partial(_resid_ln_kernel, n=n),
        grid=(n + 1,),
        in_specs=[pl.BlockSpec((tm, k), row_in),
                  _resident((k, D_MODEL), fixed),
                  pl.BlockSpec((tm, D_MODEL), row_in),
                  _resident((1, D_MODEL), fixed),
                  _resident((1, D_MODEL), fixed)],
        out_specs=[pl.BlockSpec((tm, D_MODEL), row_out) for _ in out_shape],
        out_shape=out_shape,
        scratch_shapes=[pltpu.VMEM((tm, D_MODEL), F32)],
        compiler_params=_params(1),
        name="proj_resid_ln",
    )(u, w, resid, g, b)


def _swiglu_kernel(h_ref, wg_ref, wu_ref, o_ref, wg_bf, wu_bf):
    @pl.when(pl.program_id(1) == 0)
    def _():
        wg_bf[...] = wg_ref[...].astype(BF16)
        wu_bf[...] = wu_ref[...].astype(BF16)

    h = h_ref[...]
    gate = _dot(h, wg_bf[...])
    up = _dot(h, wu_bf[...])
    o_ref[...] = (gate * jax.nn.sigmoid(gate) * up).astype(BF16)


def _swiglu_in(hb, w_ffn_in, tm=1024, tn=512):
    t = hb.shape[0]
    nj = D_FF // tn
    return pl.pallas_call(
        _swiglu_kernel,
        grid=(nj, t // tm),
        in_specs=[pl.BlockSpec((tm, D_MODEL), lambda j, i: (i, 0)),
                  pl.BlockSpec((D_MODEL, tn), lambda j, i: (0, j)),
                  pl.BlockSpec((D_MODEL, tn), lambda j, i: (0, nj + j))],
        out_specs=pl.BlockSpec((tm, tn), lambda j, i: (i, j)),
        out_shape=jax.ShapeDtypeStruct((t, D_FF), BF16),
        scratch_shapes=[pltpu.VMEM((D_MODEL, tn), BF16), pltpu.VMEM((D_MODEL, tn), BF16)],
        compiler_params=_params(2),
        name="swiglu_in",
    )(hb, w_ffn_in, w_ffn_in)


def _rope_tables(seq):
    inv = 1.0 / (ROPE_THETA ** (jnp.arange(ROPE_HALF, dtype=jnp.float32) / ROPE_HALF))
    ang = jnp.arange(seq, dtype=jnp.float32)[:, None] * inv[None, :]
    cos, sin = jnp.cos(ang), jnp.sin(ang)
    z = jnp.zeros_like(cos)
    return (jnp.concatenate([cos, cos, z, z], axis=-1),
            jnp.concatenate([-sin, z, z, z], axis=-1),
            jnp.concatenate([z, sin, z, z], axis=-1))


def _prepare_weights(w_in, w_uq, w_ukv):
    parts = np.cumsum((A_HEADS * HEAD_DIM, A_KV_HEADS * HEAD_DIM, A_KV_HEADS * HEAD_DIM, Q_LORA, KV_LORA, QK_ROPE))
    qa, ka, va, cq, ckv, kr, qc = jnp.split(w_in, [int(p) for p in parts], axis=-1)
    pad = jnp.zeros((D_MODEL, 128 - QK_ROPE), w_in.dtype)
    w_in_p = jnp.concatenate([qa, ka, va, cq, ckv, qc, kr, pad], axis=-1).astype(BF16)
    w_uq_p = jnp.pad(w_uq.reshape(Q_LORA, B_HEADS, QK_NOPE + QK_ROPE),
                     ((0, 0), (0, 0), (0, QK_PAD - QK_NOPE - QK_ROPE))).reshape(Q_LORA, B_HEADS * QK_PAD).astype(BF16)
    kv = w_ukv.reshape(KV_LORA, B_HEADS, 2, QK_NOPE)
    w_ukv_p = jnp.concatenate([kv[:, :, 0].reshape(KV_LORA, -1), kv[:, :, 1].reshape(KV_LORA, -1)],
                              axis=-1).astype(BF16)
    return w_in_p, w_uq_p, w_ukv_p


def _encoder_layer(x, mem, bias, w):
    b, seq, d = x.shape
    t = b * seq
    x2 = x.reshape(t, d)
    rope_c, rope_sa, rope_sb = _rope_tables(seq)
    xb, qa, ka, va, qc, qm, km, vt = _project(
        x2, w["w_in"], w["q_norm_g"], w["w_uq"], w["kv_norm_g"], w["w_ukv"], rope_c, rope_sa, rope_sb, seq)
    r3 = lambda v: v.reshape(b, seq, v.shape[-1])
    a_out = _window_attention(r3(qa), r3(ka), r3(va), bias, w["sink"])
    b_out = _mla_attention(r3(qm), r3(km), vt)
    mkv = _memory_kv(mem.reshape(b * N_MEM, d), w["w_mem_kv"]).reshape(b, N_MEM, -1)
    c_out = _cross_attention(r3(qc), mkv)
    merged = _gated_merge(xb, a_out.reshape(t, -1), b_out.reshape(t, -1), c_out.reshape(t, -1),
                          w["w_gate"], w["b_gate"], w["w_br_a"], w["w_br_b"], w["w_br_c"])
    h, hb = _proj_resid_ln(merged, w["w_o"], x2, w["ln1_g"], w["ln1_b"], tm=512, with_bf16=True)
    act = _swiglu_in(hb, w["w_ffn_in"])
    (y,) = _proj_resid_ln(act, w["w_ffn_down"], h, w["ln2_g"], w["ln2_b"], tm=256, with_bf16=False)
    return y.reshape(b, seq, d)


def kernel(x_prompt, x_sample, mem_prompt, mem_sample, w_in, rel_bias, sink, q_norm_g, w_uq, kv_norm_g, w_ukv,
           w_mem_kv, w_gate, b_gate, w_br_a, w_br_b, w_br_c, w_o, ln1_g, ln1_b, w_ffn_in, w_ffn_down, ln2_g, ln2_b):
    bias = _window_bias(rel_bias)
    xp, xs = x_prompt, x_sample
    for l in range(DEPTH):
        w_in_p, w_uq_p, w_ukv_p = _prepare_weights(w_in[l], w_uq[l], w_ukv[l])
        w = dict(
            w_in=w_in_p, w_uq=w_uq_p, w_ukv=w_ukv_p, sink=sink[l],
            q_norm_g=q_norm_g[l][None, :], kv_norm_g=kv_norm_g[l][None, :],
            w_mem_kv=w_mem_kv[l].astype(BF16), w_gate=w_gate[l].astype(BF16), b_gate=b_gate[l][None, :],
            w_br_a=w_br_a[l].astype(BF16), w_br_b=w_br_b[l].astype(BF16), w_br_c=w_br_c[l].astype(BF16),
            w_o=w_o[l].astype(BF16), ln1_g=ln1_g[l][None, :], ln1_b=ln1_b[l][None, :],
            w_ffn_in=w_ffn_in[l], w_ffn_down=w_ffn_down[l].astype(BF16),
            ln2_g=ln2_g[l][None, :], ln2_b=ln2_b[l][None, :])
        xp = _encoder_layer(xp, mem_prompt, bias, w)
        xs = _encoder_layer(xs, mem_sample, bias, w)
    return (xp, xs)
```

```python
import functools
import math

import jax
import jax.numpy as jnp
import numpy as np
from jax import lax
from jax.experimental import pallas as pl
from jax.experimental.pallas import tpu as pltpu

F32 = jnp.float32
BF16 = jnp.bfloat16

D_MODEL = 2048
HEAD_DIM = 128
A_HEADS = 6
A_KV_HEADS = 2
A_GROUP = A_HEADS // A_KV_HEADS
WINDOW = 128
BLOCK = 128
N_BUCKETS = 32
MAX_DISTANCE = 128
B_HEADS = 6
Q_LORA = 512
KV_LORA = 512
QK_NOPE = 128
QK_ROPE = 64
V_DIM = 128
ROPE_THETA = 10000.0
C_HEADS = 4
N_MEM = 256
N_BRANCH = 3
D_FF = -(-8 * D_MODEL // (3 * 256)) * 256
DEPTH = 1
ALPHA = (2 * DEPTH) ** 0.25
LN_EPS = 1e-5
RMS_EPS = 1e-6
NEG = -1e30

QK_PAD = 256
ROPE_HALF = QK_ROPE // 2
V_ROWS = V_DIM + 16
VMEM_LIMIT = 56 * 1024 * 1024

_OFF_QA = 0
_OFF_KA = _OFF_QA + A_HEADS * HEAD_DIM
_OFF_VA = _OFF_KA + A_KV_HEADS * HEAD_DIM
_OFF_CQ = _OFF_VA + A_KV_HEADS * HEAD_DIM
_OFF_CKV = _OFF_CQ + Q_LORA
_OFF_QC = _OFF_CKV + KV_LORA
_OFF_KR = _OFF_QC + C_HEADS * HEAD_DIM
D_IN_PAD = _OFF_KR + 128


def _params(n_axes):
    return pltpu.CompilerParams(dimension_semantics=("arbitrary",) * n_axes,
                                vmem_limit_bytes=VMEM_LIMIT)


def _resident(shape, index_map):
    return pl.BlockSpec(shape, index_map, pipeline_mode=pl.Buffered(1))


def _dot(a, b):
    return jnp.dot(a, b, preferred_element_type=F32)


def _dot_nt(a, b):
    return lax.dot_general(a, b, (((1,), (1,)), ((), ())), preferred_element_type=F32)


def _layer_norm(v, g, b):
    mu = jnp.mean(v, axis=-1, keepdims=True)
    c = v - mu
    var = jnp.mean(c * c, axis=-1, keepdims=True)
    return c * lax.rsqrt(var + LN_EPS) * g + b


def _rms_norm(v, g):
    ms = jnp.mean(v * v, axis=-1, keepdims=True)
    return v * lax.rsqrt(ms + RMS_EPS) * g


def _rope_tile(t, c, sa, sb):
    return t * c + pltpu.roll(t, 128 - ROPE_HALF, 1) * sa + pltpu.roll(t, ROPE_HALF, 1) * sb


def _proj_kernel(x_ref, w_in_ref, qg_ref, w_uq_ref, kvg_ref, w_ukv_ref, c_ref, sa_ref, sb_ref,
                 xb_ref, qa_ref, ka_ref, va_ref, qc_ref, qm_ref, km_ref, vm_ref):
    xb = x_ref[...].astype(BF16)
    xb_ref[...] = xb
    lat = _dot(xb, w_in_ref[:, _OFF_CQ:_OFF_QC])
    cqn = _rms_norm(lat[:, :Q_LORA], qg_ref[...]).astype(BF16)
    ckvn = _rms_norm(lat[:, Q_LORA:], kvg_ref[...]).astype(BF16)
    head = _dot(xb, w_in_ref[:, :_OFF_CQ])
    tail = _dot(xb, w_in_ref[:, _OFF_QC:])
    qa_ref[...] = (head[:, _OFF_QA:_OFF_KA] * (HEAD_DIM ** -0.5)).astype(BF16)
    ka_ref[...] = head[:, _OFF_KA:_OFF_VA].astype(BF16)
    va_ref[...] = head[:, _OFF_VA:_OFF_CQ].astype(BF16)
    qc_ref[...] = (tail[:, :_OFF_KR - _OFF_QC] * (HEAD_DIM ** -0.5)).astype(BF16)

    c, sa, sb = c_ref[...], sa_ref[...], sb_ref[...]
    qb = _dot(cqn, w_uq_ref[...])
    kvb = _dot(ckvn, w_ukv_ref[...])
    kr = _rope_tile(tail[:, _OFF_KR - _OFF_QC:], c, sa, sb).astype(BF16)
    scale = (QK_NOPE + QK_ROPE) ** -0.5 * math.log2(math.e)
    for h in range(B_HEADS):
        lo = h * QK_PAD
        qm_ref[:, lo:lo + QK_NOPE] = (qb[:, lo:lo + QK_NOPE] * scale).astype(BF16)
        qm_ref[:, lo + QK_NOPE:lo + QK_PAD] = (
            _rope_tile(qb[:, lo + QK_NOPE:lo + QK_PAD], c, sa, sb) * scale).astype(BF16)
        km_ref[:, lo:lo + QK_NOPE] = kvb[:, h * QK_NOPE:(h + 1) * QK_NOPE].astype(BF16)
        km_ref[:, lo + QK_NOPE:lo + QK_PAD] = kr
    ones = jnp.ones((V_ROWS - V_DIM, kvb.shape[0]), BF16)
    for h in range(B_HEADS):
        lo = (B_HEADS + h) * QK_NOPE
        vm_ref[h, :V_DIM] = kvb[:, lo:lo + V_DIM].T.astype(BF16)
        vm_ref[h, V_DIM:] = ones


def _project(x2, w_in, q_norm_g, w_uq, kv_norm_g, w_ukv, rope_c, rope_sa, rope_sb, seq, tm=256):
    t = x2.shape[0]
    n_pos = seq // tm
    row = lambda i: (i, 0)
    fixed = lambda i: (0, 0)
    pos = lambda i: (i % n_pos, 0)
    widths = (D_MODEL, A_HEADS * HEAD_DIM, A_KV_HEADS * HEAD_DIM, A_KV_HEADS * HEAD_DIM,
              C_HEADS * HEAD_DIM, B_HEADS * QK_PAD, B_HEADS * QK_PAD)
    out_specs = [pl.BlockSpec((tm, w), row) for w in widths]
    out_shape = [jax.ShapeDtypeStruct((t, w), BF16) for w in widths]
    out_specs.append(pl.BlockSpec((B_HEADS, V_ROWS, tm), lambda i: (0, 0, i)))
    out_shape.append(jax.ShapeDtypeStruct((B_HEADS, V_ROWS, t), BF16))
    return pl.pallas_call(
        _proj_kernel,
        grid=(t // tm,),
        in_specs=[
            pl.BlockSpec((tm, D_MODEL), row),
            _resident((D_MODEL, D_IN_PAD), fixed),
            _resident((1, Q_LORA), fixed),
            _resident((Q_LORA, B_HEADS * QK_PAD), fixed),
            _resident((1, KV_LORA), fixed),
            _resident((KV_LORA, B_HEADS * (QK_NOPE + V_DIM)), fixed),
            pl.BlockSpec((tm, 128), pos),
            pl.BlockSpec((tm, 128), pos),
            pl.BlockSpec((tm, 128), pos),
        ],
        out_specs=out_specs,
        out_shape=out_shape,
        compiler_params=_params(1),
        name="proj",
    )(x2, w_in, q_norm_g, w_uq, kv_norm_g, w_ukv, rope_c, rope_sa, rope_sb)


def _bias_kernel(rb_ref, bucket_ref, band_ref, o_ref):
    bucket = bucket_ref[...]
    band = band_ref[...] != 0
    for h in range(A_HEADS):
        acc = jnp.zeros(bucket.shape, F32)
        for b in range(N_BUCKETS):
            acc = jnp.where(bucket == b, rb_ref[b, h], acc)
        o_ref[h] = jnp.where(band, acc, NEG)


def _t5_bucket(rel):
    half = N_BUCKETS // 2
    max_exact = half // 2
    ret = (rel > 0).astype(jnp.int32) * half
    n = jnp.abs(rel)
    large = max_exact + (jnp.log(jnp.maximum(n, 1).astype(jnp.float32) / max_exact)
                         / math.log(MAX_DISTANCE / max_exact) * (half - max_exact)).astype(jnp.int32)
    large = jnp.minimum(large, half - 1)
    return ret + jnp.where(n < max_exact, n, large)


def _window_bias(rel_bias):
    rel = (jnp.arange(3 * BLOCK) - BLOCK)[None, :] - jnp.arange(BLOCK)[:, None]
    bucket = _t5_bucket(rel).astype(jnp.int32)
    band = (jnp.abs(rel) <= WINDOW).astype(jnp.int32)
    return pl.pallas_call(
        _bias_kernel,
        in_specs=[pl.BlockSpec(memory_space=pltpu.SMEM),
                  pl.BlockSpec(memory_space=pltpu.VMEM),
                  pl.BlockSpec(memory_space=pltpu.VMEM)],
        out_specs=pl.BlockSpec(memory_space=pltpu.VMEM),
        out_shape=jax.ShapeDtypeStruct((A_HEADS, BLOCK, 3 * BLOCK), F32),
        name="win_bias",
    )(rel_bias, bucket, band)


def _win_kernel(sink_ref, q_ref, kp_ref, kc_ref, kn_ref, vp_ref, vc_ref, vn_ref, bias_ref, o_ref,
                kbuf, vbuf, *, tq, seq):
    i = pl.program_id(1)
    kbuf[0:BLOCK] = kp_ref[...]
    kbuf[BLOCK:BLOCK + tq] = kc_ref[...]
    kbuf[BLOCK + tq:] = kn_ref[...]
    vbuf[0:BLOCK] = vp_ref[...]
    vbuf[BLOCK:BLOCK + tq] = vc_ref[...]
    vbuf[BLOCK + tq:] = vn_ref[...]
    rows = A_GROUP * BLOCK
    n_sub = tq // BLOCK
    n_tiles = seq // tq
    col = lax.broadcasted_iota(jnp.int32, (rows, 3 * BLOCK), 1)
    row_head = lax.broadcasted_iota(jnp.int32, (rows, 1), 0) // BLOCK
    for g in range(A_KV_HEADS):
        sink = jnp.zeros((rows, 1), F32)
        for r in range(A_GROUP):
            sink = jnp.where(row_head == r, sink_ref[g * A_GROUP + r], sink)
        bias = bias_ref[g * A_GROUP:(g + 1) * A_GROUP].reshape(rows, 3 * BLOCK)
        for j in range(n_sub):
            kw = kbuf[j * BLOCK:(j + 3) * BLOCK, g * HEAD_DIM:(g + 1) * HEAD_DIM]
            vw = vbuf[j * BLOCK:(j + 3) * BLOCK, g * HEAD_DIM:(g + 1) * HEAD_DIM]
            q = jnp.concatenate(
                [q_ref[j * BLOCK:(j + 1) * BLOCK, (g * A_GROUP + r) * HEAD_DIM:(g * A_GROUP + r + 1) * HEAD_DIM]
                 for r in range(A_GROUP)], axis=0)
            s = _dot_nt(q, kw) + bias
            if j == 0:
                s = jnp.where(jnp.logical_and(i == 0, col < BLOCK), NEG, s)
            if j == n_sub - 1:
                s = jnp.where(jnp.logical_and(i == n_tiles - 1, col >= 2 * BLOCK), NEG, s)
            m = jnp.maximum(jnp.max(s, axis=-1, keepdims=True), sink)
            p = jnp.exp(s - m)
            den = jnp.sum(p, axis=-1, keepdims=True) + jnp.exp(sink - m)
            o = _dot(p.astype(BF16), vw) / den
            for r in range(A_GROUP):
                h = g * A_GROUP + r
                o_ref[j * BLOCK:(j + 1) * BLOCK, h * HEAD_DIM:(h + 1) * HEAD_DIM] = (
                    o[r * BLOCK:(r + 1) * BLOCK].astype(BF16))


def _window_attention(qa, ka, va, bias, sink, tq=512):
    b, seq, _ = qa.shape
    nb = seq // BLOCK
    per = tq // BLOCK
    kvw = A_KV_HEADS * HEAD_DIM
    prev = lambda bi, i: (bi, jnp.maximum(i * per - 1, 0), 0)
    cur = lambda bi, i: (bi, i, 0)
    nxt = lambda bi, i: (bi, jnp.minimum((i + 1) * per, nb - 1), 0)
    return pl.pallas_call(
        functools.partial(_win_kernel, tq=tq, seq=seq),
        grid=(b, seq // tq),
        in_specs=[
            pl.BlockSpec(memory_space=pltpu.SMEM),
            pl.BlockSpec((None, tq, A_HEADS * HEAD_DIM), cur),
            pl.BlockSpec((None, BLOCK, kvw), prev),
            pl.BlockSpec((None, tq, kvw), cur),
            pl.BlockSpec((None, BLOCK, kvw), nxt),
            pl.BlockSpec((None, BLOCK, kvw), prev),
            pl.BlockSpec((None, tq, kvw), cur),
            pl.BlockSpec((None, BLOCK, kvw), nxt),
            _resident((A_HEADS, BLOCK, 3 * BLOCK), lambda bi, i: (0, 0, 0)),
        ],
        out_specs=pl.BlockSpec((None, tq, A_HEADS * HEAD_DIM), cur),
        out_shape=jax.ShapeDtypeStruct((b, seq, A_HEADS * HEAD_DIM), BF16),
        scratch_shapes=[pltpu.VMEM((tq + 2 * BLOCK, kvw), BF16),
                        pltpu.VMEM((tq + 2 * BLOCK, kvw), BF16)],
        compiler_params=_params(2),
        name="window_attn",
    )(sink, qa, ka, ka, ka, va, va, va, bias)


def _mla_kernel(q_ref, k_ref, vt_ref, o_ref, s_buf, p_buf, qt_ref, m_ref, *, tk, nk, nq):
    i = pl.program_id(2)
    tq = q_ref.shape[0]
    grp = tk // 8
    neg_inf = jnp.full((8, tq), -jnp.inf, F32)

    def transpose_q():
        qt_ref[...] = q_ref[...].T

    def new_scores(c):
        st = _dot(k_ref[c * tk:(c + 1) * tk, :], qt_ref[...])
        s_buf[c * tk:(c + 1) * tk, :] = st
        return jnp.max(st.reshape(grp, 8, tq), axis=0)

    def exp_scores(c, m8):
        p = jnp.exp2(s_buf[c * tk:(c + 1) * tk, :].reshape(grp, 8, tq) - m8[None])
        p_buf[c * tk:(c + 1) * tk, :] = p.reshape(tk, tq).astype(BF16)

    def set_max(mx):
        m_ref[...] = jnp.broadcast_to(jnp.max(mx, axis=0, keepdims=True), (8, tq))

    @pl.when(i == 0)
    def _():
        transpose_q()
        mx = neg_inf
        for c in range(nk):
            mx = jnp.maximum(mx, new_scores(c))
        set_max(mx)

    @pl.when((i > 0) & (i < nq))
    def _():
        transpose_q()
        m8 = m_ref[...]
        mx = neg_inf
        exp_scores(0, m8)
        for c in range(1, nk):
            exp_scores(c, m8)
            mx = jnp.maximum(mx, new_scores(c - 1))
        set_max(jnp.maximum(mx, new_scores(nk - 1)))

    @pl.when(i == nq)
    def _():
        m8 = m_ref[...]
        for c in range(nk):
            exp_scores(c, m8)

    @pl.when(i > 0)
    def _():
        pv = _dot(vt_ref[...], p_buf[...])
        o_t = pv[:V_DIM] / pv[V_DIM:V_DIM + 1]
        o_ref[...] = o_t.T.astype(o_ref.dtype)


def _mla_attention(qm, km, vt, tq=256, tk=1024):
    b, seq, _ = qm.shape
    nq = seq // tq
    assert seq % tk == 0 and seq % tq == 0
    return pl.pallas_call(
        functools.partial(_mla_kernel, tk=tk, nk=seq // tk, nq=nq),
        grid=(b, B_HEADS, nq + 1),
        in_specs=[
            pl.BlockSpec((None, tq, QK_PAD), lambda bi, h, i: (bi, jnp.minimum(i, nq - 1), h)),
            pl.BlockSpec((None, seq, QK_PAD), lambda bi, h, i: (bi, 0, h)),
            pl.BlockSpec((None, V_ROWS, seq), lambda bi, h, i: (h, 0, bi)),
        ],
        out_specs=pl.BlockSpec((None, tq, V_DIM), lambda bi, h, i: (bi, jnp.maximum(i - 1, 0), h)),
        out_shape=jax.ShapeDtypeStruct((b, seq, B_HEADS * V_DIM), BF16),
        scratch_shapes=[pltpu.VMEM((seq, tq), F32), pltpu.VMEM((seq, tq), BF16), pltpu.VMEM((QK_PAD, tq), BF16),
                        pltpu.VMEM((8, tq), F32)],
        compiler_params=_params(3),
        name="mla_attn",
    )(qm, km, vt)


def _memkv_kernel(m_ref, w_ref, o_ref):
    o_ref[...] = _dot(m_ref[...].astype(BF16), w_ref[...]).astype(BF16)


def _memory_kv(mem2, w_mem_kv):
    t = mem2.shape[0]
    n = w_mem_kv.shape[1]
    return pl.pallas_call(
        _memkv_kernel,
        grid=(t // N_MEM,),
        in_specs=[pl.BlockSpec((N_MEM, D_MODEL), lambda i: (i, 0)),
                  _resident((D_MODEL, n), lambda i: (0, 0))],
        out_specs=pl.BlockSpec((N_MEM, n), lambda i: (i, 0)),
        out_shape=jax.ShapeDtypeStruct((t, n), BF16),
        compiler_params=_params(1),
        name="mem_kv",
    )(mem2, w_mem_kv)


def _cross_kernel(q_ref, k_ref, v_ref, o_ref):
    for h in range(C_HEADS):
        sl = slice(h * HEAD_DIM, (h + 1) * HEAD_DIM)
        s = _dot_nt(q_ref[:, sl], k_ref[:, sl])
        m = jnp.max(s, axis=-1, keepdims=True)
        p = jnp.exp(s - m)
        den = jnp.sum(p, axis=-1, keepdims=True)
        o_ref[:, sl] = (_dot(p.astype(BF16), v_ref[:, sl]) / den).astype(BF16)


def _cross_attention(qc, mkv, tq=512):
    b, seq, w = qc.shape
    return pl.pallas_call(
        _cross_kernel,
        grid=(b, seq // tq),
        in_specs=[pl.BlockSpec((None, tq, w), lambda bi, i: (bi, i, 0)),
                  pl.BlockSpec((None, N_MEM, w), lambda bi, i: (bi, 0, 0)),
                  pl.BlockSpec((None, N_MEM, w), lambda bi, i: (bi, 0, 1))],
        out_specs=pl.BlockSpec((None, tq, w), lambda bi, i: (bi, i, 0)),
        out_shape=jax.ShapeDtypeStruct((b, seq, w), BF16),
        compiler_params=_params(2),
        name="cross_attn",
    )(qc, mkv, mkv)


def _merge_kernel(xb_ref, a_ref, b_ref, c_ref, wg0_ref, wg1_ref, wg2_ref, bg0_ref, bg1_ref, bg2_ref,
                  wa_ref, wb_ref, wc_ref, o_ref):
    xb = xb_ref[...]
    acc = jax.nn.sigmoid(_dot(xb, wg0_ref[...]) + bg0_ref[...]) * _dot(a_ref[...], wa_ref[...])
    acc += jax.nn.sigmoid(_dot(xb, wg1_ref[...]) + bg1_ref[...]) * _dot(b_ref[...], wb_ref[...])
    acc += jax.nn.sigmoid(_dot(xb, wg2_ref[...]) + bg2_ref[...]) * _dot(c_ref[...], wc_ref[...])
    o_ref[...] = acc.astype(BF16)


def _gated_merge(xb, a, b, c, w_gate, b_gate, w_br_a, w_br_b, w_br_c, tm=1024, tn=512):
    t = xb.shape[0]
    nj = D_MODEL // tn
    row = lambda i, j: (i, 0)
    colj = lambda i, j: (0, j)
    gate = lambda n: (lambda i, j: (0, n * nj + j))
    return pl.pallas_call(
        _merge_kernel,
        grid=(t // tm, nj),
        in_specs=[
            pl.BlockSpec((tm, D_MODEL), row),
            pl.BlockSpec((tm, a.shape[1]), row),
            pl.BlockSpec((tm, b.shape[1]), row),
            pl.BlockSpec((tm, c.shape[1]), row),
            pl.BlockSpec((D_MODEL, tn), gate(0)),
            pl.BlockSpec((D_MODEL, tn), gate(1)),
            pl.BlockSpec((D_MODEL, tn), gate(2)),
            pl.BlockSpec((1, tn), gate(0)),
            pl.BlockSpec((1, tn), gate(1)),
            pl.BlockSpec((1, tn), gate(2)),
            pl.BlockSpec((a.shape[1], tn), colj),
            pl.BlockSpec((b.shape[1], tn), colj),
            pl.BlockSpec((c.shape[1], tn), colj),
        ],
        out_specs=pl.BlockSpec((tm, tn), lambda i, j: (i, j)),
        out_shape=jax.ShapeDtypeStruct((t, D_MODEL), BF16),
        compiler_params=_params(2),
        name="gated_merge",
    )(xb, a, b, c, w_gate, w_gate, w_gate, b_gate, b_gate, b_gate, w_br_a, w_br_b, w_br_c)


def _resid_ln_kernel(u_ref, w_ref, r_ref, g_ref, b_ref, *refs, n):
    *o_refs, z_ref = refs
    i = pl.program_id(0)

    def matmul():
        z_ref[...] = ALPHA * r_ref[...] + _dot(u_ref[...], w_ref[...])

    def norm():
        y = _layer_norm(z_ref[...], g_ref[...], b_ref[...])
        o_refs[0][...] = y
        if len(o_refs) > 1:
            o_refs[1][...] = y.astype(BF16)

    @pl.when(i == 0)
    def _():
        matmul()

    @pl.when((i > 0) & (i < n))
    def _():
        norm()
        matmul()

    @pl.when(i == n)
    def _():
        norm()


def _proj_resid_ln(u, w, resid, g, b, tm, with_bf16):
    t, k = u.shape
    n = t // tm
    row_in = lambda i: (jnp.minimum(i, n - 1), 0)
    row_out = lambda i: (jnp.maximum(i - 1, 0), 0)
    fixed = lambda i: (0, 0)
    out_shape = [jax.ShapeDtypeStruct((t, D_MODEL), F32)]
    if with_bf16:
        out_shape.append(jax.ShapeDtypeStruct((t, D_MODEL), BF16))
    return pl.pallas_call(
        functools.partial(_resid_ln_kernel, n=n),
        grid=(n + 1,),
        in_specs=[pl.BlockSpec((tm, k), row_in),
                  _resident((k, D_MODEL), fixed),
                  pl.BlockSpec((tm, D_MODEL), row_in),
                  _resident((1, D_MODEL), fixed),
                  _resident((1, D_MODEL), fixed)],
        out_specs=[pl.BlockSpec((tm, D_MODEL), row_out) for _ in out_shape],
        out_shape=out_shape,
        scratch_shapes=[pltpu.VMEM((tm, D_MODEL), F32)],
        compiler_params=_params(1),
        name="proj_resid_ln",
    )(u, w, resid, g, b)


def _swiglu_kernel(h_ref, wg_ref, wu_ref, o_ref, wg_bf, wu_bf):
    @pl.when(pl.program_id(1) == 0)
    def _():
        wg_bf[...] = wg_ref[...].astype(BF16)
        wu_bf[...] = wu_ref[...].astype(BF16)

    h = h_ref[...]
    gate = _dot(h, wg_bf[...])
    up = _dot(h, wu_bf[...])
    o_ref[...] = (gate * jax.nn.sigmoid(gate) * up).astype(BF16)


def _swiglu_in(hb, w_ffn_in, tm=1024, tn=512):
    t = hb.shape[0]
    nj = D_FF // tn
    return pl.pallas_call(
        _swiglu_kernel,
        grid=(nj, t // tm),
        in_specs=[pl.BlockSpec((tm, D_MODEL), lambda j, i: (i, 0)),
                  pl.BlockSpec((D_MODEL, tn), lambda j, i: (0, j)),
                  pl.BlockSpec((D_MODEL, tn), lambda j, i: (0, nj + j))],
        out_specs=pl.BlockSpec((tm, tn), lambda j, i: (i, j)),
        out_shape=jax.ShapeDtypeStruct((t, D_FF), BF16),
        scratch_shapes=[pltpu.VMEM((D_MODEL, tn), BF16), pltpu.VMEM((D_MODEL, tn), BF16)],
        compiler_params=_params(2),
        name="swiglu_in",
    )(hb, w_ffn_in, w_ffn_in)


def _rope_tables(seq):
    inv = 1.0 / (ROPE_THETA ** (jnp.arange(ROPE_HALF, dtype=jnp.float32) / ROPE_HALF))
    ang = jnp.arange(seq, dtype=jnp.float32)[:, None] * inv[None, :]
    cos, sin = jnp.cos(ang), jnp.sin(ang)
    z = jnp.zeros_like(cos)
    return (jnp.concatenate([cos, cos, z, z], axis=-1),
            jnp.concatenate([-sin, z, z, z], axis=-1),
            jnp.concatenate([z, sin, z, z], axis=-1))


def _prepare_weights(w_in, w_uq, w_ukv):
    parts = np.cumsum((A_HEADS * HEAD_DIM, A_KV_HEADS * HEAD_DIM, A_KV_HEADS * HEAD_DIM, Q_LORA, KV_LORA, QK_ROPE))
    qa, ka, va, cq, ckv, kr, qc = jnp.split(w_in, [int(p) for p in parts], axis=-1)
    pad = jnp.zeros((D_MODEL, 128 - QK_ROPE), w_in.dtype)
    w_in_p = jnp.concatenate([p.astype(BF16) for p in (qa, ka, va, cq, ckv, qc, kr, pad)], axis=-1)
    w_uq_p = jnp.pad(w_uq.reshape(Q_LORA, B_HEADS, QK_NOPE + QK_ROPE),
                     ((0, 0), (0, 0), (0, QK_PAD - QK_NOPE - QK_ROPE))).reshape(Q_LORA, B_HEADS * QK_PAD).astype(BF16)
    kv = w_ukv.reshape(KV_LORA, B_HEADS, 2, QK_NOPE)
    w_ukv_p = jnp.concatenate([kv[:, :, 0].reshape(KV_LORA, -1), kv[:, :, 1].reshape(KV_LORA, -1)],
                              axis=-1).astype(BF16)
    return w_in_p, w_uq_p, w_ukv_p


def _encoder_layer(x, mem, bias, w):
    b, seq, d = x.shape
    t = b * seq
    x2 = x.reshape(t, d)
    rope_c, rope_sa, rope_sb = _rope_tables(seq)
    xb, qa, ka, va, qc, qm, km, vt = _project(
        x2, w["w_in"], w["q_norm_g"], w["w_uq"], w["kv_norm_g"], w["w_ukv"], rope_c, rope_sa, rope_sb, seq)
    r3 = lambda v: v.reshape(b, seq, v.shape[-1])
    a_out = _window_attention(r3(qa), r3(ka), r3(va), bias, w["sink"])
    b_out = _mla_attention(r3(qm), r3(km), vt)
    mkv = _memory_kv(mem.reshape(b * N_MEM, d), w["w_mem_kv"]).reshape(b, N_MEM, -1)
    c_out = _cross_attention(r3(qc), mkv)
    merged = _gated_merge(xb, a_out.reshape(t, -1), b_out.reshape(t, -1), c_out.reshape(t, -1),
                          w["w_gate"], w["b_gate"], w["w_br_a"], w["w_br_b"], w["w_br_c"])
    h, hb = _proj_resid_ln(merged, w["w_o"], x2, w["ln1_g"], w["ln1_b"], tm=512, with_bf16=True)
    act = _swiglu_in(hb, w["w_ffn_in"])
    (y,) = _proj_resid_ln(act, w["w_ffn_down"], h, w["ln2_g"], w["ln2_b"], tm=256, with_bf16=False)
    return y.reshape(b, seq, d)


def kernel(x_prompt, x_sample, mem_prompt, mem_sample, w_in, rel_bias, sink, q_norm_g, w_uq, kv_norm_g, w_ukv,
           w_mem_kv, w_gate, b_gate, w_br_a, w_br_b, w_br_c, w_o, ln1_g, ln1_b, w_ffn_in, w_ffn_down, ln2_g, ln2_b):
    bias = _window_bias(rel_bias)
    xp, xs = x_prompt, x_sample
    for l in range(DEPTH):
        w_in_p, w_uq_p, w_ukv_p = _prepare_weights(w_in[l], w_uq[l], w_ukv[l])
        w = dict(
            w_in=w_in_p, w_uq=w_uq_p, w_ukv=w_ukv_p, sink=sink[l],
            q_norm_g=q_norm_g[l][None, :], kv_norm_g=kv_norm_g[l][None, :],
            w_mem_kv=w_mem_kv[l].astype(BF16), w_gate=w_gate[l].astype(BF16), b_gate=b_gate[l][None, :],
            w_br_a=w_br_a[l].astype(BF16), w_br_b=w_br_b[l].astype(BF16), w_br_c=w_br_c[l].astype(BF16),
            w_o=w_o[l].astype(BF16), ln1_g=ln1_g[l][None, :], ln1_b=ln1_b[l][None, :],
            w_ffn_in=w_ffn_in[l], w_ffn_down=w_ffn_down[l].astype(BF16),
            ln2_g=ln2_g[l][None, :], ln2_b=ln2_b[l][None, :])
        xp = _encoder_layer(xp, mem_prompt, bias, w)
        xs = _encoder_layer(xs, mem_sample, bias, w)
    return (xp, xs)
```

```python
import functools
import math

import jax
import jax.numpy as jnp
import numpy as np
from jax import lax
from jax.experimental import pallas as pl
from jax.experimental.pallas import tpu as pltpu

F32 = jnp.float32
BF16 = jnp.bfloat16

D_MODEL = 2048
HEAD_DIM = 128
A_HEADS = 6
A_KV_HEADS = 2
A_GROUP = A_HEADS // A_KV_HEADS
WINDOW = 128
BLOCK = 128
N_BUCKETS = 32
MAX_DISTANCE = 128
B_HEADS = 6
Q_LORA = 512
KV_LORA = 512
QK_NOPE = 128
QK_ROPE = 64
V_DIM = 128
ROPE_THETA = 10000.0
C_HEADS = 4
N_MEM = 256
N_BRANCH = 3
D_FF = -(-8 * D_MODEL // (3 * 256)) * 256
DEPTH = 1
ALPHA = (2 * DEPTH) ** 0.25
LN_EPS = 1e-5
RMS_EPS = 1e-6
NEG = -1e30

VMEM_LIMIT = 56 * 1024 * 1024
MXU_K = 256
LANES = 128
BF16_ROWS = 16

QK_PAD = 2 * LANES
ROPE_HALF = QK_ROPE // 2
V_ROWS = V_DIM + BF16_ROWS

TM_PROJ = 256
TQ_WINDOW = 512
TQ_MLA = 256
TK_MLA = 1024
TQ_CROSS = 512
TM_MERGE, TN_MERGE = 1024, 512
TM_OUT_PROJ = 512
TM_SWIGLU, TN_SWIGLU = 1024, 512
TM_FFN_DOWN = 256

_OFF_QA = 0
_OFF_KA = _OFF_QA + A_HEADS * HEAD_DIM
_OFF_VA = _OFF_KA + A_KV_HEADS * HEAD_DIM
_OFF_CQ = _OFF_VA + A_KV_HEADS * HEAD_DIM
_OFF_CKV = _OFF_CQ + Q_LORA
_OFF_QC = _OFF_CKV + KV_LORA
_OFF_KR = _OFF_QC + C_HEADS * HEAD_DIM
D_IN_PAD = _OFF_KR + LANES


def _params(n_axes):
    return pltpu.CompilerParams(dimension_semantics=("arbitrary",) * n_axes,
                                vmem_limit_bytes=VMEM_LIMIT)


def _resident(shape, index_map):
    return pl.BlockSpec(shape, index_map, pipeline_mode=pl.Buffered(1))


def _dot(a, b):
    return jnp.dot(a, b, preferred_element_type=F32)


def _dot_nt(a, b):
    return lax.dot_general(a, b, (((1,), (1,)), ((), ())), preferred_element_type=F32)


def _layer_norm(v, g, b):
    mu = jnp.mean(v, axis=-1, keepdims=True)
    c = v - mu
    var = jnp.mean(c * c, axis=-1, keepdims=True)
    return c * lax.rsqrt(var + LN_EPS) * g + b


def _rms_norm(v, g):
    ms = jnp.mean(v * v, axis=-1, keepdims=True)
    return v * lax.rsqrt(ms + RMS_EPS) * g


def _rope_tile(t, c, sa, sb):
    return t * c + pltpu.roll(t, LANES - ROPE_HALF, 1) * sa + pltpu.roll(t, ROPE_HALF, 1) * sb


def _proj_kernel(x_ref, w_in_ref, qg_ref, w_uq_ref, kvg_ref, w_ukv_ref, c_ref, sa_ref, sb_ref,
                 xb_ref, qa_ref, ka_ref, va_ref, qc_ref, qm_ref, km_ref, vm_ref):
    xb = x_ref[...].astype(BF16)
    xb_ref[...] = xb
    lat = _dot(xb, w_in_ref[:, _OFF_CQ:_OFF_QC])
    cqn = _rms_norm(lat[:, :Q_LORA], qg_ref[...]).astype(BF16)
    ckvn = _rms_norm(lat[:, Q_LORA:], kvg_ref[...]).astype(BF16)
    head = _dot(xb, w_in_ref[:, :_OFF_CQ])
    tail = _dot(xb, w_in_ref[:, _OFF_QC:])
    qa_ref[...] = (head[:, _OFF_QA:_OFF_KA] * (HEAD_DIM ** -0.5)).astype(BF16)
    ka_ref[...] = head[:, _OFF_KA:_OFF_VA].astype(BF16)
    va_ref[...] = head[:, _OFF_VA:_OFF_CQ].astype(BF16)
    qc_ref[...] = (tail[:, :_OFF_KR - _OFF_QC] * (HEAD_DIM ** -0.5)).astype(BF16)

    c, sa, sb = c_ref[...], sa_ref[...], sb_ref[...]
    qb = _dot(cqn, w_uq_ref[...])
    kvb = _dot(ckvn, w_ukv_ref[...])
    kr = _rope_tile(tail[:, _OFF_KR - _OFF_QC:], c, sa, sb).astype(BF16)
    scale = (QK_NOPE + QK_ROPE) ** -0.5 * math.log2(math.e)
    for h in range(B_HEADS):
        lo = h * QK_PAD
        qm_ref[:, lo:lo + QK_NOPE] = (qb[:, lo:lo + QK_NOPE] * scale).astype(BF16)
        qm_ref[:, lo + QK_NOPE:lo + QK_PAD] = (
            _rope_tile(qb[:, lo + QK_NOPE:lo + QK_PAD], c, sa, sb) * scale).astype(BF16)
        km_ref[:, lo:lo + QK_NOPE] = kvb[:, h * QK_NOPE:(h + 1) * QK_NOPE].astype(BF16)
        km_ref[:, lo + QK_NOPE:lo + QK_PAD] = kr
    ones = jnp.ones((V_ROWS - V_DIM, kvb.shape[0]), BF16)
    for h in range(B_HEADS):
        lo = (B_HEADS + h) * QK_NOPE
        vm_ref[h, :V_DIM] = kvb[:, lo:lo + V_DIM].T.astype(BF16)
        vm_ref[h, V_DIM:] = ones


def _project(x2, w_in, q_norm_g, w_uq, kv_norm_g, w_ukv, rope_c, rope_sa, rope_sb, seq, tm=TM_PROJ):
    t = x2.shape[0]
    n_pos = seq // tm
    row = lambda i: (i, 0)
    fixed = lambda i: (0, 0)
    pos = lambda i: (i % n_pos, 0)
    widths = (D_MODEL, A_HEADS * HEAD_DIM, A_KV_HEADS * HEAD_DIM, A_KV_HEADS * HEAD_DIM,
              C_HEADS * HEAD_DIM, B_HEADS * QK_PAD, B_HEADS * QK_PAD)
    out_specs = [pl.BlockSpec((tm, w), row) for w in widths]
    out_shape = [jax.ShapeDtypeStruct((t, w), BF16) for w in widths]
    out_specs.append(pl.BlockSpec((B_HEADS, V_ROWS, tm), lambda i: (0, 0, i)))
    out_shape.append(jax.ShapeDtypeStruct((B_HEADS, V_ROWS, t), BF16))
    return pl.pallas_call(
        _proj_kernel,
        grid=(t // tm,),
        in_specs=[
            pl.BlockSpec((tm, D_MODEL), row),
            _resident((D_MODEL, D_IN_PAD), fixed),
            _resident((1, Q_LORA), fixed),
            _resident((Q_LORA, B_HEADS * QK_PAD), fixed),
            _resident((1, KV_LORA), fixed),
            _resident((KV_LORA, B_HEADS * (QK_NOPE + V_DIM)), fixed),
            pl.BlockSpec((tm, LANES), pos),
            pl.BlockSpec((tm, LANES), pos),
            pl.BlockSpec((tm, LANES), pos),
        ],
        out_specs=out_specs,
        out_shape=out_shape,
        compiler_params=_params(1),
        name="proj",
    )(x2, w_in, q_norm_g, w_uq, kv_norm_g, w_ukv, rope_c, rope_sa, rope_sb)


def _bias_kernel(rb_ref, bucket_ref, band_ref, o_ref):
    bucket = bucket_ref[...]
    band = band_ref[...] != 0
    for h in range(A_HEADS):
        acc = jnp.zeros(bucket.shape, F32)
        for b in range(N_BUCKETS):
            acc = jnp.where(bucket == b, rb_ref[b, h], acc)
        o_ref[h] = jnp.where(band, acc, NEG)


def _t5_bucket(rel):
    half = N_BUCKETS // 2
    max_exact = half // 2
    ret = (rel > 0).astype(jnp.int32) * half
    n = jnp.abs(rel)
    large = max_exact + (jnp.log(jnp.maximum(n, 1).astype(jnp.float32) / max_exact)
                         / math.log(MAX_DISTANCE / max_exact) * (half - max_exact)).astype(jnp.int32)
    large = jnp.minimum(large, half - 1)
    return ret + jnp.where(n < max_exact, n, large)


def _window_bias(rel_bias):
    rel = (jnp.arange(3 * BLOCK) - BLOCK)[None, :] - jnp.arange(BLOCK)[:, None]
    bucket = _t5_bucket(rel).astype(jnp.int32)
    band = (jnp.abs(rel) <= WINDOW).astype(jnp.int32)
    return pl.pallas_call(
        _bias_kernel,
        in_specs=[pl.BlockSpec(memory_space=pltpu.SMEM),
                  pl.BlockSpec(memory_space=pltpu.VMEM),
                  pl.BlockSpec(memory_space=pltpu.VMEM)],
        out_specs=pl.BlockSpec(memory_space=pltpu.VMEM),
        out_shape=jax.ShapeDtypeStruct((A_HEADS, BLOCK, 3 * BLOCK), F32),
        name="win_bias",
    )(rel_bias, bucket, band)


def _win_kernel(sink_ref, q_ref, kp_ref, kc_ref, kn_ref, vp_ref, vc_ref, vn_ref, bias_ref, o_ref,
                kbuf, vbuf, *, tq, seq):
    i = pl.program_id(1)
    kbuf[0:BLOCK] = kp_ref[...]
    kbuf[BLOCK:BLOCK + tq] = kc_ref[...]
    kbuf[BLOCK + tq:] = kn_ref[...]
    vbuf[0:BLOCK] = vp_ref[...]
    vbuf[BLOCK:BLOCK + tq] = vc_ref[...]
    vbuf[BLOCK + tq:] = vn_ref[...]
    rows = A_GROUP * BLOCK
    n_sub = tq // BLOCK
    n_tiles = seq // tq
    col = lax.broadcasted_iota(jnp.int32, (rows, 3 * BLOCK), 1)
    row_head = lax.broadcasted_iota(jnp.int32, (rows, 1), 0) // BLOCK
    for g in range(A_KV_HEADS):
        sink = jnp.zeros((rows, 1), F32)
        for r in range(A_GROUP):
            sink = jnp.where(row_head == r, sink_ref[g * A_GROUP + r], sink)
        bias = bias_ref[g * A_GROUP:(g + 1) * A_GROUP].reshape(rows, 3 * BLOCK)
        for j in range(n_sub):
            kw = kbuf[j * BLOCK:(j + 3) * BLOCK, g * HEAD_DIM:(g + 1) * HEAD_DIM]
            vw = vbuf[j * BLOCK:(j + 3) * BLOCK, g * HEAD_DIM:(g + 1) * HEAD_DIM]
            q = jnp.concatenate(
                [q_ref[j * BLOCK:(j + 1) * BLOCK, (g * A_GROUP + r) * HEAD_DIM:(g * A_GROUP + r + 1) * HEAD_DIM]
                 for r in range(A_GROUP)], axis=0)
            s = _dot_nt(q, kw) + bias
            if j == 0:
                s = jnp.where(jnp.logical_and(i == 0, col < BLOCK), NEG, s)
            if j == n_sub - 1:
                s = jnp.where(jnp.logical_and(i == n_tiles - 1, col >= 2 * BLOCK), NEG, s)
            m = jnp.maximum(jnp.max(s, axis=-1, keepdims=True), sink)
            p = jnp.exp(s - m)
            den = jnp.sum(p, axis=-1, keepdims=True) + jnp.exp(sink - m)
            o = _dot(p.astype(BF16), vw) / den
            for r in range(A_GROUP):
                h = g * A_GROUP + r
                o_ref[j * BLOCK:(j + 1) * BLOCK, h * HEAD_DIM:(h + 1) * HEAD_DIM] = (
                    o[r * BLOCK:(r + 1) * BLOCK].astype(BF16))


def _window_attention(qa, ka, va, bias, sink, tq=TQ_WINDOW):
    b, seq, _ = qa.shape
    nb = seq // BLOCK
    per = tq // BLOCK
    kvw = A_KV_HEADS * HEAD_DIM
    prev = lambda bi, i: (bi, jnp.maximum(i * per - 1, 0), 0)
    cur = lambda bi, i: (bi, i, 0)
    nxt = lambda bi, i: (bi, jnp.minimum((i + 1) * per, nb - 1), 0)
    return pl.pallas_call(
        functools.partial(_win_kernel, tq=tq, seq=seq),
        grid=(b, seq // tq),
        in_specs=[
            pl.BlockSpec(memory_space=pltpu.SMEM),
            pl.BlockSpec((None, tq, A_HEADS * HEAD_DIM), cur),
            pl.BlockSpec((None, BLOCK, kvw), prev),
            pl.BlockSpec((None, tq, kvw), cur),
            pl.BlockSpec((None, BLOCK, kvw), nxt),
            pl.BlockSpec((None, BLOCK, kvw), prev),
            pl.BlockSpec((None, tq, kvw), cur),
            pl.BlockSpec((None, BLOCK, kvw), nxt),
            _resident((A_HEADS, BLOCK, 3 * BLOCK), lambda bi, i: (0, 0, 0)),
        ],
        out_specs=pl.BlockSpec((None, tq, A_HEADS * HEAD_DIM), cur),
        out_shape=jax.ShapeDtypeStruct((b, seq, A_HEADS * HEAD_DIM), BF16),
        scratch_shapes=[pltpu.VMEM((tq + 2 * BLOCK, kvw), BF16),
                        pltpu.VMEM((tq + 2 * BLOCK, kvw), BF16)],
        compiler_params=_params(2),
        name="window_attn",
    )(sink, qa, ka, ka, ka, va, va, va, bias)


def _mla_kernel(q_ref, k_ref, vt_ref, o_ref, s_buf, p_buf, qt_ref, m_ref, acc_ref, *, tk, nk, nq):
    i = pl.program_id(2)
    tq = q_ref.shape[0]
    grp = tk // 8
    neg_inf = jnp.full((8, tq), -jnp.inf, F32)

    def transpose_q():
        qt_ref[...] = q_ref[...].T

    def new_scores(c):
        st = _dot(k_ref[c * tk:(c + 1) * tk, :], qt_ref[...])
        s_buf[c * tk:(c + 1) * tk, :] = st
        return jnp.max(st.reshape(grp, 8, tq), axis=0)

    def exp_scores(c, m8):
        p = jnp.exp2(s_buf[c * tk:(c + 1) * tk, :].reshape(grp, 8, tq) - m8[None])
        p_buf[c * tk:(c + 1) * tk, :] = p.reshape(tk, tq).astype(BF16)

    def set_max(mx):
        m_ref[...] = jnp.broadcast_to(jnp.max(mx, axis=0, keepdims=True), (8, tq))

    def pv_chunk(c):
        part = None
        for t in range(c * tk, (c + 1) * tk, MXU_K):
            d = _dot(vt_ref[:, t:t + MXU_K], p_buf[t:t + MXU_K, :])
            part = d if part is None else part + d
        acc_ref[...] += part

    def emit():
        pv = acc_ref[...]
        o_t = pv[:V_DIM] / pv[V_DIM:V_DIM + 1]
        o_ref[...] = o_t.T.astype(o_ref.dtype)

    @pl.when(i == 0)
    def _():
        transpose_q()
        mx = neg_inf
        for c in range(nk):
            mx = jnp.maximum(mx, new_scores(c))
        set_max(mx)

    @pl.when((i > 0) & (i < nq))
    def _():
        transpose_q()
        m8 = m_ref[...]
        mx = neg_inf
        acc_ref[...] = jnp.zeros_like(acc_ref)
        exp_scores(0, m8)
        for c in range(1, nk):
            exp_scores(c, m8)
            mx = jnp.maximum(mx, new_scores(c - 1))
            pv_chunk(c - 1)
        set_max(jnp.maximum(mx, new_scores(nk - 1)))
        pv_chunk(nk - 1)

    @pl.when(i == nq)
    def _():
        m8 = m_ref[...]
        acc_ref[...] = jnp.zeros_like(acc_ref)
        for c in range(nk):
            exp_scores(c, m8)
            pv_chunk(c)

    @pl.when(i > 0)
    def _():
        emit()


def _mla_attention(qm, km, vt, tq=TQ_MLA, tk=TK_MLA):
    b, seq, _ = qm.shape
    nq = seq // tq
    assert seq % tk == 0 and seq % tq == 0 and tk % MXU_K == 0
    return pl.pallas_call(
        functools.partial(_mla_kernel, tk=tk, nk=seq // tk, nq=nq),
        grid=(b, B_HEADS, nq + 1),
        in_specs=[
            pl.BlockSpec((None, tq, QK_PAD), lambda bi, h, i: (bi, jnp.minimum(i, nq - 1), h)),
            pl.BlockSpec((None, seq, QK_PAD), lambda bi, h, i: (bi, 0, h)),
            pl.BlockSpec((None, V_ROWS, seq), lambda bi, h, i: (h, 0, bi)),
        ],
        out_specs=pl.BlockSpec((None, tq, V_DIM), lambda bi, h, i: (bi, jnp.maximum(i - 1, 0), h)),
        out_shape=jax.ShapeDtypeStruct((b, seq, B_HEADS * V_DIM), BF16),
        scratch_shapes=[pltpu.VMEM((seq, tq), F32), pltpu.VMEM((seq, tq), BF16), pltpu.VMEM((QK_PAD, tq), BF16),
                        pltpu.VMEM((8, tq), F32), pltpu.VMEM((V_ROWS, tq), F32)],
        compiler_params=_params(3),
        name="mla_attn",
    )(qm, km, vt)


def _memkv_kernel(m_ref, w_ref, o_ref):
    o_ref[...] = _dot(m_ref[...].astype(BF16), w_ref[...]).astype(BF16)


def _memory_kv(mem2, w_mem_kv):
    t = mem2.shape[0]
    n = w_mem_kv.shape[1]
    return pl.pallas_call(
        _memkv_kernel,
        grid=(t // N_MEM,),
        in_specs=[pl.BlockSpec((N_MEM, D_MODEL), lambda i: (i, 0)),
                  _resident((D_MODEL, n), lambda i: (0, 0))],
        out_specs=pl.BlockSpec((N_MEM, n), lambda i: (i, 0)),
        out_shape=jax.ShapeDtypeStruct((t, n), BF16),
        compiler_params=_params(1),
        name="mem_kv",
    )(mem2, w_mem_kv)


def _cross_kernel(q_ref, k_ref, v_ref, o_ref):
    for h in range(C_HEADS):
        sl = slice(h * HEAD_DIM, (h + 1) * HEAD_DIM)
        s = _dot_nt(q_ref[:, sl], k_ref[:, sl])
        m = jnp.max(s, axis=-1, keepdims=True)
        p = jnp.exp(s - m)
        den = jnp.sum(p, axis=-1, keepdims=True)
        o_ref[:, sl] = (_dot(p.astype(BF16), v_ref[:, sl]) / den).astype(BF16)


def _cross_attention(qc, mkv, tq=TQ_CROSS):
    b, seq, w = qc.shape
    return pl.pallas_call(
        _cross_kernel,
        grid=(b, seq // tq),
        in_specs=[pl.BlockSpec((None, tq, w), lambda bi, i: (bi, i, 0)),
                  pl.BlockSpec((None, N_MEM, w), lambda bi, i: (bi, 0, 0)),
                  pl.BlockSpec((None, N_MEM, w), lambda bi, i: (bi, 0, 1))],
        out_specs=pl.BlockSpec((None, tq, w), lambda bi, i: (bi, i, 0)),
        out_shape=jax.ShapeDtypeStruct((b, seq, w), BF16),
        compiler_params=_params(2),
        name="cross_attn",
    )(qc, mkv, mkv)


def _merge_kernel(xb_ref, a_ref, b_ref, c_ref, wg0_ref, wg1_ref, wg2_ref, bg0_ref, bg1_ref, bg2_ref,
                  wa_ref, wb_ref, wc_ref, o_ref):
    xb = xb_ref[...]
    acc = jax.nn.sigmoid(_dot(xb, wg0_ref[...]) + bg0_ref[...]) * _dot(a_ref[...], wa_ref[...])
    acc += jax.nn.sigmoid(_dot(xb, wg1_ref[...]) + bg1_ref[...]) * _dot(b_ref[...], wb_ref[...])
    acc += jax.nn.sigmoid(_dot(xb, wg2_ref[...]) + bg2_ref[...]) * _dot(c_ref[...], wc_ref[...])
    o_ref[...] = acc.astype(BF16)


def _gated_merge(xb, a, b, c, w_gate, b_gate, w_br_a, w_br_b, w_br_c, tm=TM_MERGE, tn=TN_MERGE):
    t = xb.shape[0]
    nj = D_MODEL // tn
    row = lambda i, j: (i, 0)
    colj = lambda i, j: (0, j)
    gate = lambda n: (lambda i, j: (0, n * nj + j))
    return pl.pallas_call(
        _merge_kernel,
        grid=(t // tm, nj),
        in_specs=[
            pl.BlockSpec((tm, D_MODEL), row),
            pl.BlockSpec((tm, a.shape[1]), row),
            pl.BlockSpec((tm, b.shape[1]), row),
            pl.BlockSpec((tm, c.shape[1]), row),
            pl.BlockSpec((D_MODEL, tn), gate(0)),
            pl.BlockSpec((D_MODEL, tn), gate(1)),
            pl.BlockSpec((D_MODEL, tn), gate(2)),
            pl.BlockSpec((1, tn), gate(0)),
            pl.BlockSpec((1, tn), gate(1)),
            pl.BlockSpec((1, tn), gate(2)),
            pl.BlockSpec((a.shape[1], tn), colj),
            pl.BlockSpec((b.shape[1], tn), colj),
            pl.BlockSpec((c.shape[1], tn), colj),
        ],
        out_specs=pl.BlockSpec((tm, tn), lambda i, j: (i, j)),
        out_shape=jax.ShapeDtypeStruct((t, D_MODEL), BF16),
        compiler_params=_params(2),
        name="gated_merge",
    )(xb, a, b, c, w_gate, w_gate, w_gate, b_gate, b_gate, b_gate, w_br_a, w_br_b, w_br_c)


def _resid_ln_kernel(u_ref, w_ref, r_ref, g_ref, b_ref, *refs, n):
    *o_refs, z_ref = refs
    i = pl.program_id(0)

    def matmul():
        z_ref[...] = ALPHA * r_ref[...] + _dot(u_ref[...], w_ref[...])

    def norm():
        y = _layer_norm(z_ref[...], g_ref[...], b_ref[...])
        o_refs[0][...] = y
        if len(o_refs) > 1:
            o_refs[1][...] = y.astype(BF16)

    @pl.when(i == 0)
    def _():
        matmul()

    @pl.when((i > 0) & (i < n))
    def _():
        norm()
        matmul()

    @pl.when(i == n)
    def _():
        norm()


def _proj_resid_ln(u, w, resid, g, b, tm, with_bf16):
    t, k = u.shape
    n = t // tm
    row_in = lambda i: (jnp.minimum(i, n - 1), 0)
    row_out = lambda i: (jnp.maximum(i - 1, 0), 0)
    fixed = lambda i: (0, 0)
    out_shape = [jax.ShapeDtypeStruct((t, D_MODEL), F32)]
    if with_bf16:
        out_shape.append(jax.ShapeDtypeStruct((t, D_MODEL), BF16))
    return pl.pallas_call(
        functools.partial(_resid_ln_kernel, n=n),
        grid=(n + 1,),
        in_specs=[pl.BlockSpec((tm, k), row_in),
                  _resident((k, D_MODEL), fixed),
                  pl.BlockSpec((tm, D_MODEL), row_in),
                  _resident((1, D_MODEL), fixed),
                  _resident((1, D_MODEL), fixed)],
        out_specs=[pl.BlockSpec((tm, D_MODEL), row_out) for _ in out_shape],
        out_shape=out_shape,
        scratch_shapes=[pltpu.VMEM((tm, D_MODEL), F32)],
        compiler_params=_params(1),
        name="proj_resid_ln",
    )(u, w, resid, g, b)


def _swiglu_kernel(h_ref, wg_ref, wu_ref, o_ref, wg_bf, wu_bf):
    @pl.when(pl.program_id(1) == 0)
    def _():
        wg_bf[...] = wg_ref[...].astype(BF16)
        wu_bf[...] = wu_ref[...].astype(BF16)

    h = h_ref[...]
    gate = _dot(h, wg_bf[...])
    up = _dot(h, wu_bf[...])
    o_ref[...] = (gate * jax.nn.sigmoid(gate) * up).astype(BF16)


def _swiglu_in(hb, w_ffn_in, tm=TM_SWIGLU, tn=TN_SWIGLU):
    t = hb.shape[0]
    nj = D_FF // tn
    return pl.pallas_call(
        _swiglu_kernel,
        grid=(nj, t // tm),
        in_specs=[pl.BlockSpec((tm, D_MODEL), lambda j, i: (i, 0)),
                  pl.BlockSpec((D_MODEL, tn), lambda j, i: (0, j)),
                  pl.BlockSpec((D_MODEL, tn), lambda j, i: (0, nj + j))],
        out_specs=pl.BlockSpec((tm, tn), lambda j, i: (i, j)),
        out_shape=jax.ShapeDtypeStruct((t, D_FF), BF16),
        scratch_shapes=[pltpu.VMEM((D_MODEL, tn), BF16), pltpu.VMEM((D_MODEL, tn), BF16)],
        compiler_params=_params(2),
        name="swiglu_in",
    )(hb, w_ffn_in, w_ffn_in)


def _rope_tables(seq):
    inv = 1.0 / (ROPE_THETA ** (jnp.arange(ROPE_HALF, dtype=jnp.float32) / ROPE_HALF))
    ang = jnp.arange(seq, dtype=jnp.float32)[:, None] * inv[None, :]
    cos, sin = jnp.cos(ang), jnp.sin(ang)
    z = jnp.zeros_like(cos)
    return (jnp.concatenate([cos, cos, z, z], axis=-1),
            jnp.concatenate([-sin, z, z, z], axis=-1),
            jnp.concatenate([z, sin, z, z], axis=-1))


def _prepare_weights(w_in, w_uq, w_ukv):
    parts = np.cumsum((A_HEADS * HEAD_DIM, A_KV_HEADS * HEAD_DIM, A_KV_HEADS * HEAD_DIM, Q_LORA, KV_LORA, QK_ROPE))
    qa, ka, va, cq, ckv, kr, qc = jnp.split(w_in, [int(p) for p in parts], axis=-1)
    pad = jnp.zeros((D_MODEL, LANES - QK_ROPE), w_in.dtype)
    w_in_p = jnp.concatenate([p.astype(BF16) for p in (qa, ka, va, cq, ckv, qc, kr, pad)], axis=-1)
    w_uq_p = jnp.pad(w_uq.reshape(Q_LORA, B_HEADS, QK_NOPE + QK_ROPE),
                     ((0, 0), (0, 0), (0, QK_PAD - QK_NOPE - QK_ROPE))).reshape(Q_LORA, B_HEADS * QK_PAD).astype(BF16)
    kv = w_ukv.reshape(KV_LORA, B_HEADS, 2, QK_NOPE)
    w_ukv_p = jnp.concatenate([kv[:, :, 0].reshape(KV_LORA, -1), kv[:, :, 1].reshape(KV_LORA, -1)],
                              axis=-1).astype(BF16)
    return w_in_p, w_uq_p, w_ukv_p


def _encoder_layer(x, mem, bias, w):
    b, seq, d = x.shape
    t = b * seq
    x2 = x.reshape(t, d)
    rope_c, rope_sa, rope_sb = _rope_tables(seq)
    xb, qa, ka, va, qc, qm, km, vt = _project(
        x2, w["w_in"], w["q_norm_g"], w["w_uq"], w["kv_norm_g"], w["w_ukv"], rope_c, rope_sa, rope_sb, seq)
    r3 = lambda v: v.reshape(b, seq, v.shape[-1])
    a_out = _window_attention(r3(qa), r3(ka), r3(va), bias, w["sink"])
    b_out = _mla_attention(r3(qm), r3(km), vt)
    mkv = _memory_kv(mem.reshape(b * N_MEM, d), w["w_mem_kv"]).reshape(b, N_MEM, -1)
    c_out = _cross_attention(r3(qc), mkv)
    merged = _gated_merge(xb, a_out.reshape(t, -1), b_out.reshape(t, -1), c_out.reshape(t, -1),
                          w["w_gate"], w["b_gate"], w["w_br_a"], w["w_br_b"], w["w_br_c"])
    h, hb = _proj_resid_ln(merged, w["w_o"], x2, w["ln1_g"], w["ln1_b"], tm=TM_OUT_PROJ, with_bf16=True)
    act = _swiglu_in(hb, w["w_ffn_in"])
    (y,) = _proj_resid_ln(act, w["w_ffn_down"], h, w["ln2_g"], w["ln2_b"], tm=TM_FFN_DOWN, with_bf16=False)
    return y.reshape(b, seq, d)


def kernel(x_prompt, x_sample, mem_prompt, mem_sample, w_in, rel_bias, sink, q_norm_g, w_uq, kv_norm_g, w_ukv,
           w_mem_kv, w_gate, b_gate, w_br_a, w_br_b, w_br_c, w_o, ln1_g, ln1_b, w_ffn_in, w_ffn_down, ln2_g, ln2_b):
    bias = _window_bias(rel_bias)
    xp, xs = x_prompt, x_sample
    for l in range(DEPTH):
        w_in_p, w_uq_p, w_ukv_p = _prepare_weights(w_in[l], w_uq[l], w_ukv[l])
        w = dict(
            w_in=w_in_p, w_uq=w_uq_p, w_ukv=w_ukv_p, sink=sink[l],
            q_norm_g=q_norm_g[l][None, :], kv_norm_g=kv_norm_g[l][None, :],
            w_mem_kv=w_mem_kv[l].astype(BF16), w_gate=w_gate[l].astype(BF16), b_gate=b_gate[l][None, :],
            w_br_a=w_br_a[l].astype(BF16), w_br_b=w_br_b[l].astype(BF16), w_br_c=w_br_c[l].astype(BF16),
            w_o=w_o[l].astype(BF16), ln1_g=ln1_g[l][None, :], ln1_b=ln1_b[l][None, :],
            w_ffn_in=w_ffn_in[l], w_ffn_down=w_ffn_down[l].astype(BF16),
            ln2_g=ln2_g[l][None, :], ln2_b=ln2_b[l][None, :])
        xp = _encoder_layer(xp, mem_prompt, bias, w)
        xs = _encoder_layer(xs, mem_sample, bias, w)
    return (xp, xs)
```

```python
import functools
import math

import jax
import jax.numpy as jnp
import numpy as np
from jax import lax
from jax.experimental import pallas as pl
from jax.experimental.pallas import tpu as pltpu

F32 = jnp.float32
BF16 = jnp.bfloat16

D_MODEL = 2048
HEAD_DIM = 128
A_HEADS = 6
A_KV_HEADS = 2
A_GROUP = A_HEADS // A_KV_HEADS
WINDOW = 128
BLOCK = 128
N_BUCKETS = 32
MAX_DISTANCE = 128
B_HEADS = 6
Q_LORA = 512
KV_LORA = 512
QK_NOPE = 128
QK_ROPE = 64
V_DIM = 128
ROPE_THETA = 10000.0
C_HEADS = 4
N_MEM = 256
N_BRANCH = 3
D_FF = -(-8 * D_MODEL // (3 * 256)) * 256
DEPTH = 1
ALPHA = (2 * DEPTH) ** 0.25
LN_EPS = 1e-5
RMS_EPS = 1e-6
NEG = -1e30

VMEM_LIMIT = 56 * 1024 * 1024
MXU_K = 256
LANES = 128
BF16_ROWS = 16

QK_PAD = 2 * LANES
ROPE_HALF = QK_ROPE // 2
V_ROWS = V_DIM + BF16_ROWS

TM_PROJ = 256
TQ_WINDOW = 512
TQ_MLA = 256
TK_MLA = 1024
TM_MERGE, TN_MERGE = 1024, 512
TM_OUT_PROJ = 512
TM_SWIGLU, TN_SWIGLU = 1024, 512
TM_FFN_DOWN = 256

_OFF_QA = 0
_OFF_KA = _OFF_QA + A_HEADS * HEAD_DIM
_OFF_VA = _OFF_KA + A_KV_HEADS * HEAD_DIM
_OFF_CQ = _OFF_VA + A_KV_HEADS * HEAD_DIM
_OFF_CKV = _OFF_CQ + Q_LORA
_OFF_QC = _OFF_CKV + KV_LORA
_OFF_KR = _OFF_QC + C_HEADS * HEAD_DIM
D_IN_PAD = _OFF_KR + LANES


def _params(n_axes):
    return pltpu.CompilerParams(dimension_semantics=("arbitrary",) * n_axes,
                                vmem_limit_bytes=VMEM_LIMIT)


def _resident(shape, index_map):
    return pl.BlockSpec(shape, index_map, pipeline_mode=pl.Buffered(1))


def _dot(a, b):
    return jnp.dot(a, b, preferred_element_type=F32)


def _dot_nt(a, b):
    return lax.dot_general(a, b, (((1,), (1,)), ((), ())), preferred_element_type=F32)


def _layer_norm(v, g, b):
    mu = jnp.mean(v, axis=-1, keepdims=True)
    c = v - mu
    var = jnp.mean(c * c, axis=-1, keepdims=True)
    return c * lax.rsqrt(var + LN_EPS) * g + b


def _rms_norm(v, g):
    ms = jnp.mean(v * v, axis=-1, keepdims=True)
    return v * lax.rsqrt(ms + RMS_EPS) * g


def _rope_tile(t, c, sa, sb):
    return t * c + pltpu.roll(t, LANES - ROPE_HALF, 1) * sa + pltpu.roll(t, ROPE_HALF, 1) * sb


def _proj_kernel(x_ref, w_in_ref, qg_ref, w_uq_ref, kvg_ref, w_ukv_ref, c_ref, sa_ref, sb_ref,
                 xb_ref, qa_ref, ka_ref, va_ref, qc_ref, qm_ref, km_ref, vm_ref):
    xb = x_ref[...].astype(BF16)
    xb_ref[...] = xb
    lat = _dot(xb, w_in_ref[:, _OFF_CQ:_OFF_QC])
    cqn = _rms_norm(lat[:, :Q_LORA], qg_ref[...]).astype(BF16)
    ckvn = _rms_norm(lat[:, Q_LORA:], kvg_ref[...]).astype(BF16)
    head = _dot(xb, w_in_ref[:, :_OFF_CQ])
    tail = _dot(xb, w_in_ref[:, _OFF_QC:])
    qa_ref[...] = (head[:, _OFF_QA:_OFF_KA] * (HEAD_DIM ** -0.5)).astype(BF16)
    ka_ref[...] = head[:, _OFF_KA:_OFF_VA].astype(BF16)
    va_ref[...] = head[:, _OFF_VA:_OFF_CQ].astype(BF16)
    qc_ref[...] = (tail[:, :_OFF_KR - _OFF_QC] * (HEAD_DIM ** -0.5)).astype(BF16)

    c, sa, sb = c_ref[...], sa_ref[...], sb_ref[...]
    qb = _dot(cqn, w_uq_ref[...])
    kvb = _dot(ckvn, w_ukv_ref[...])
    kr = _rope_tile(tail[:, _OFF_KR - _OFF_QC:], c, sa, sb).astype(BF16)
    scale = (QK_NOPE + QK_ROPE) ** -0.5 * math.log2(math.e)
    for h in range(B_HEADS):
        lo = h * QK_PAD
        qm_ref[:, lo:lo + QK_NOPE] = (qb[:, lo:lo + QK_NOPE] * scale).astype(BF16)
        qm_ref[:, lo + QK_NOPE:lo + QK_PAD] = (
            _rope_tile(qb[:, lo + QK_NOPE:lo + QK_PAD], c, sa, sb) * scale).astype(BF16)
        km_ref[:, lo:lo + QK_NOPE] = kvb[:, h * QK_NOPE:(h + 1) * QK_NOPE].astype(BF16)
        km_ref[:, lo + QK_NOPE:lo + QK_PAD] = kr
    ones = jnp.ones((V_ROWS - V_DIM, kvb.shape[0]), BF16)
    for h in range(B_HEADS):
        lo = (B_HEADS + h) * QK_NOPE
        vm_ref[h, :V_DIM] = kvb[:, lo:lo + V_DIM].T.astype(BF16)
        vm_ref[h, V_DIM:] = ones


def _project(x2, w_in, q_norm_g, w_uq, kv_norm_g, w_ukv, rope_c, rope_sa, rope_sb, seq, tm=TM_PROJ):
    t = x2.shape[0]
    n_pos = seq // tm
    row = lambda i: (i, 0)
    fixed = lambda i: (0, 0)
    pos = lambda i: (i % n_pos, 0)
    widths = (D_MODEL, A_HEADS * HEAD_DIM, A_KV_HEADS * HEAD_DIM, A_KV_HEADS * HEAD_DIM,
              C_HEADS * HEAD_DIM, B_HEADS * QK_PAD, B_HEADS * QK_PAD)
    out_specs = [pl.BlockSpec((tm, w), row) for w in widths]
    out_shape = [jax.ShapeDtypeStruct((t, w), BF16) for w in widths]
    out_specs.append(pl.BlockSpec((B_HEADS, V_ROWS, tm), lambda i: (0, 0, i)))
    out_shape.append(jax.ShapeDtypeStruct((B_HEADS, V_ROWS, t), BF16))
    return pl.pallas_call(
        _proj_kernel,
        grid=(t // tm,),
        in_specs=[
            pl.BlockSpec((tm, D_MODEL), row),
            _resident((D_MODEL, D_IN_PAD), fixed),
            _resident((1, Q_LORA), fixed),
            _resident((Q_LORA, B_HEADS * QK_PAD), fixed),
            _resident((1, KV_LORA), fixed),
            _resident((KV_LORA, B_HEADS * (QK_NOPE + V_DIM)), fixed),
            pl.BlockSpec((tm, LANES), pos),
            pl.BlockSpec((tm, LANES), pos),
            pl.BlockSpec((tm, LANES), pos),
        ],
        out_specs=out_specs,
        out_shape=out_shape,
        compiler_params=_params(1),
        name="proj",
    )(x2, w_in, q_norm_g, w_uq, kv_norm_g, w_ukv, rope_c, rope_sa, rope_sb)


def _bias_kernel(rb_ref, bucket_ref, band_ref, o_ref):
    bucket = bucket_ref[...]
    band = band_ref[...] != 0
    for h in range(A_HEADS):
        acc = jnp.zeros(bucket.shape, F32)
        for b in range(N_BUCKETS):
            acc = jnp.where(bucket == b, rb_ref[b, h], acc)
        o_ref[h] = jnp.where(band, acc, NEG)


def _t5_bucket(rel):
    half = N_BUCKETS // 2
    max_exact = half // 2
    ret = (rel > 0).astype(jnp.int32) * half
    n = jnp.abs(rel)
    large = max_exact + (jnp.log(jnp.maximum(n, 1).astype(jnp.float32) / max_exact)
                         / math.log(MAX_DISTANCE / max_exact) * (half - max_exact)).astype(jnp.int32)
    large = jnp.minimum(large, half - 1)
    return ret + jnp.where(n < max_exact, n, large)


def _window_bias(rel_bias):
    rel = (jnp.arange(3 * BLOCK) - BLOCK)[None, :] - jnp.arange(BLOCK)[:, None]
    bucket = _t5_bucket(rel).astype(jnp.int32)
    band = (jnp.abs(rel) <= WINDOW).astype(jnp.int32)
    return pl.pallas_call(
        _bias_kernel,
        in_specs=[pl.BlockSpec(memory_space=pltpu.SMEM),
                  pl.BlockSpec(memory_space=pltpu.VMEM),
                  pl.BlockSpec(memory_space=pltpu.VMEM)],
        out_specs=pl.BlockSpec(memory_space=pltpu.VMEM),
        out_shape=jax.ShapeDtypeStruct((A_HEADS, BLOCK, 3 * BLOCK), F32),
        name="win_bias",
    )(rel_bias, bucket, band)


def _win_kernel(sink_ref, q_ref, kp_ref, kc_ref, kn_ref, vp_ref, vc_ref, vn_ref, bias_ref,
                qc_ref, mk_ref, mv_ref, o_ref, co_ref, kbuf, vbuf, *, tq, seq):
    _cross_kernel(qc_ref, mk_ref, mv_ref, co_ref)
    i = pl.program_id(1)
    kbuf[0:BLOCK] = kp_ref[...]
    kbuf[BLOCK:BLOCK + tq] = kc_ref[...]
    kbuf[BLOCK + tq:] = kn_ref[...]
    vbuf[0:BLOCK] = vp_ref[...]
    vbuf[BLOCK:BLOCK + tq] = vc_ref[...]
    vbuf[BLOCK + tq:] = vn_ref[...]
    rows = A_GROUP * BLOCK
    n_sub = tq // BLOCK
    n_tiles = seq // tq
    col = lax.broadcasted_iota(jnp.int32, (rows, 3 * BLOCK), 1)
    row_head = lax.broadcasted_iota(jnp.int32, (rows, 1), 0) // BLOCK
    for g in range(A_KV_HEADS):
        sink = jnp.zeros((rows, 1), F32)
        for r in range(A_GROUP):
            sink = jnp.where(row_head == r, sink_ref[g * A_GROUP + r], sink)
        bias = bias_ref[g * A_GROUP:(g + 1) * A_GROUP].reshape(rows, 3 * BLOCK)
        for j in range(n_sub):
            kw = kbuf[j * BLOCK:(j + 3) * BLOCK, g * HEAD_DIM:(g + 1) * HEAD_DIM]
            vw = vbuf[j * BLOCK:(j + 3) * BLOCK, g * HEAD_DIM:(g + 1) * HEAD_DIM]
            q = jnp.concatenate(
                [q_ref[j * BLOCK:(j + 1) * BLOCK, (g * A_GROUP + r) * HEAD_DIM:(g * A_GROUP + r + 1) * HEAD_DIM]
                 for r in range(A_GROUP)], axis=0)
            s = _dot_nt(q, kw) + bias
            if j == 0:
                s = jnp.where(jnp.logical_and(i == 0, col < BLOCK), NEG, s)
            if j == n_sub - 1:
                s = jnp.where(jnp.logical_and(i == n_tiles - 1, col >= 2 * BLOCK), NEG, s)
            m = jnp.maximum(jnp.max(s, axis=-1, keepdims=True), sink)
            p = jnp.exp(s - m)
            den = jnp.sum(p, axis=-1, keepdims=True) + jnp.exp(sink - m)
            o = _dot(p.astype(BF16), vw) / den
            for r in range(A_GROUP):
                h = g * A_GROUP + r
                o_ref[j * BLOCK:(j + 1) * BLOCK, h * HEAD_DIM:(h + 1) * HEAD_DIM] = (
                    o[r * BLOCK:(r + 1) * BLOCK].astype(BF16))


def _window_and_cross_attention(qa, ka, va, bias, sink, qc, mkv, tq=TQ_WINDOW):
    b, seq, _ = qa.shape
    cw = qc.shape[-1]
    nb = seq // BLOCK
    per = tq // BLOCK
    kvw = A_KV_HEADS * HEAD_DIM
    prev = lambda bi, i: (bi, jnp.maximum(i * per - 1, 0), 0)
    cur = lambda bi, i: (bi, i, 0)
    nxt = lambda bi, i: (bi, jnp.minimum((i + 1) * per, nb - 1), 0)
    return pl.pallas_call(
        functools.partial(_win_kernel, tq=tq, seq=seq),
        grid=(b, seq // tq),
        in_specs=[
            pl.BlockSpec(memory_space=pltpu.SMEM),
            pl.BlockSpec((None, tq, A_HEADS * HEAD_DIM), cur),
            pl.BlockSpec((None, BLOCK, kvw), prev),
            pl.BlockSpec((None, tq, kvw), cur),
            pl.BlockSpec((None, BLOCK, kvw), nxt),
            pl.BlockSpec((None, BLOCK, kvw), prev),
            pl.BlockSpec((None, tq, kvw), cur),
            pl.BlockSpec((None, BLOCK, kvw), nxt),
            _resident((A_HEADS, BLOCK, 3 * BLOCK), lambda bi, i: (0, 0, 0)),
            pl.BlockSpec((None, tq, cw), cur),
            pl.BlockSpec((None, N_MEM, cw), lambda bi, i: (bi, 0, 0)),
            pl.BlockSpec((None, N_MEM, cw), lambda bi, i: (bi, 0, 1)),
        ],
        out_specs=[pl.BlockSpec((None, tq, A_HEADS * HEAD_DIM), cur),
                   pl.BlockSpec((None, tq, cw), cur)],
        out_shape=[jax.ShapeDtypeStruct((b, seq, A_HEADS * HEAD_DIM), BF16),
                   jax.ShapeDtypeStruct((b, seq, cw), BF16)],
        scratch_shapes=[pltpu.VMEM((tq + 2 * BLOCK, kvw), BF16),
                        pltpu.VMEM((tq + 2 * BLOCK, kvw), BF16)],
        compiler_params=_params(2),
        name="window_cross_attn",
    )(sink, qa, ka, ka, ka, va, va, va, bias, qc, mkv, mkv)


def _mla_kernel(q_ref, k_ref, vt_ref, o_ref, s_buf, p_buf, qt_ref, m_ref, acc_ref, *, tk, nk, nq):
    i = pl.program_id(2)
    tq = q_ref.shape[0]
    grp = tk // 8
    neg_inf = jnp.full((8, tq), -jnp.inf, F32)

    def transpose_q():
        qt_ref[...] = q_ref[...].T

    def new_scores(c):
        st = _dot(k_ref[c * tk:(c + 1) * tk, :], qt_ref[...])
        s_buf[c * tk:(c + 1) * tk, :] = st
        return jnp.max(st.reshape(grp, 8, tq), axis=0)

    def exp_scores(c, m8):
        p = jnp.exp2(s_buf[c * tk:(c + 1) * tk, :].reshape(grp, 8, tq) - m8[None])
        p_buf[c * tk:(c + 1) * tk, :] = p.reshape(tk, tq).astype(BF16)

    def set_max(mx):
        m_ref[...] = jnp.broadcast_to(jnp.max(mx, axis=0, keepdims=True), (8, tq))

    def pv_chunk(c):
        part = None
        for t in range(c * tk, (c + 1) * tk, MXU_K):
            d = _dot(vt_ref[:, t:t + MXU_K], p_buf[t:t + MXU_K, :])
            part = d if part is None else part + d
        acc_ref[...] += part

    def emit():
        pv = acc_ref[...]
        o_t = pv[:V_DIM] / pv[V_DIM:V_DIM + 1]
        o_ref[...] = o_t.T.astype(o_ref.dtype)

    @pl.when(i == 0)
    def _():
        transpose_q()
        mx = neg_inf
        for c in range(nk):
            mx = jnp.maximum(mx, new_scores(c))
        set_max(mx)

    @pl.when((i > 0) & (i < nq))
    def _():
        transpose_q()
        m8 = m_ref[...]
        mx = neg_inf
        acc_ref[...] = jnp.zeros_like(acc_ref)
        exp_scores(0, m8)
        for c in range(1, nk):
            exp_scores(c, m8)
            mx = jnp.maximum(mx, new_scores(c - 1))
            pv_chunk(c - 1)
        pv_chunk(nk - 1)
        emit()
        set_max(jnp.maximum(mx, new_scores(nk - 1)))

    @pl.when(i == nq)
    def _():
        m8 = m_ref[...]
        acc_ref[...] = jnp.zeros_like(acc_ref)
        for c in range(nk):
            exp_scores(c, m8)
            pv_chunk(c)
        emit()


def _mla_attention(qm, km, vt, tq=TQ_MLA, tk=TK_MLA):
    b, seq, _ = qm.shape
    nq = seq // tq
    assert seq % tk == 0 and seq % tq == 0 and tk % MXU_K == 0
    return pl.pallas_call(
        functools.partial(_mla_kernel, tk=tk, nk=seq // tk, nq=nq),
        grid=(b, B_HEADS, nq + 1),
        in_specs=[
            pl.BlockSpec((None, tq, QK_PAD), lambda bi, h, i: (bi, jnp.minimum(i, nq - 1), h)),
            pl.BlockSpec((None, seq, QK_PAD), lambda bi, h, i: (bi, 0, h)),
            pl.BlockSpec((None, V_ROWS, seq), lambda bi, h, i: (h, 0, bi)),
        ],
        out_specs=pl.BlockSpec((None, tq, V_DIM), lambda bi, h, i: (bi, jnp.maximum(i - 1, 0), h)),
        out_shape=jax.ShapeDtypeStruct((b, seq, B_HEADS * V_DIM), BF16),
        scratch_shapes=[pltpu.VMEM((seq, tq), F32), pltpu.VMEM((seq, tq), BF16), pltpu.VMEM((QK_PAD, tq), BF16),
                        pltpu.VMEM((8, tq), F32), pltpu.VMEM((V_ROWS, tq), F32)],
        compiler_params=_params(3),
        name="mla_attn",
    )(qm, km, vt)


def _memkv_kernel(m_ref, w_ref, o_ref):
    o_ref[...] = _dot(m_ref[...].astype(BF16), w_ref[...]).astype(BF16)


def _memory_kv(mem2, w_mem_kv):
    t = mem2.shape[0]
    n = w_mem_kv.shape[1]
    return pl.pallas_call(
        _memkv_kernel,
        grid=(t // N_MEM,),
        in_specs=[pl.BlockSpec((N_MEM, D_MODEL), lambda i: (i, 0)),
                  _resident((D_MODEL, n), lambda i: (0, 0))],
        out_specs=pl.BlockSpec((N_MEM, n), lambda i: (i, 0)),
        out_shape=jax.ShapeDtypeStruct((t, n), BF16),
        compiler_params=_params(1),
        name="mem_kv",
    )(mem2, w_mem_kv)


def _cross_kernel(q_ref, k_ref, v_ref, o_ref):
    for h in range(C_HEADS):
        sl = slice(h * HEAD_DIM, (h + 1) * HEAD_DIM)
        s = _dot_nt(q_ref[:, sl], k_ref[:, sl])
        m = jnp.max(s, axis=-1, keepdims=True)
        p = jnp.exp(s - m)
        den = jnp.sum(p, axis=-1, keepdims=True)
        o_ref[:, sl] = (_dot(p.astype(BF16), v_ref[:, sl]) / den).astype(BF16)


def _merge_kernel(xb_ref, a_ref, b_ref, c_ref, wg0_ref, wg1_ref, wg2_ref, bg0_ref, bg1_ref, bg2_ref,
                  wa_ref, wb_ref, wc_ref, o_ref):
    xb = xb_ref[...]
    acc = jax.nn.sigmoid(_dot(xb, wg0_ref[...]) + bg0_ref[...]) * _dot(a_ref[...], wa_ref[...])
    acc += jax.nn.sigmoid(_dot(xb, wg1_ref[...]) + bg1_ref[...]) * _dot(b_ref[...], wb_ref[...])
    acc += jax.nn.sigmoid(_dot(xb, wg2_ref[...]) + bg2_ref[...]) * _dot(c_ref[...], wc_ref[...])
    o_ref[...] = acc.astype(BF16)


def _gated_merge(xb, a, b, c, w_gate, b_gate, w_br_a, w_br_b, w_br_c, tm=TM_MERGE, tn=TN_MERGE):
    t = xb.shape[0]
    nj = D_MODEL // tn
    row = lambda i, j: (i, 0)
    colj = lambda i, j: (0, j)
    gate = lambda n: (lambda i, j: (0, n * nj + j))
    return pl.pallas_call(
        _merge_kernel,
        grid=(t // tm, nj),
        in_specs=[
            pl.BlockSpec((tm, D_MODEL), row),
            pl.BlockSpec((tm, a.shape[1]), row),
            pl.BlockSpec((tm, b.shape[1]), row),
            pl.BlockSpec((tm, c.shape[1]), row),
            pl.BlockSpec((D_MODEL, tn), gate(0)),
            pl.BlockSpec((D_MODEL, tn), gate(1)),
            pl.BlockSpec((D_MODEL, tn), gate(2)),
            pl.BlockSpec((1, tn), gate(0)),
            pl.BlockSpec((1, tn), gate(1)),
            pl.BlockSpec((1, tn), gate(2)),
            pl.BlockSpec((a.shape[1], tn), colj),
            pl.BlockSpec((b.shape[1], tn), colj),
            pl.BlockSpec((c.shape[1], tn), colj),
        ],
        out_specs=pl.BlockSpec((tm, tn), lambda i, j: (i, j)),
        out_shape=jax.ShapeDtypeStruct((t, D_MODEL), BF16),
        compiler_params=_params(2),
        name="gated_merge",
    )(xb, a, b, c, w_gate, w_gate, w_gate, b_gate, b_gate, b_gate, w_br_a, w_br_b, w_br_c)


def _resid_ln_kernel(u_ref, w_ref, r_ref, g_ref, b_ref, *refs, n):
    *o_refs, z_ref = refs
    i = pl.program_id(0)

    def matmul():
        z_ref[...] = ALPHA * r_ref[...] + _dot(u_ref[...], w_ref[...])

    def norm():
        y = _layer_norm(z_ref[...], g_ref[...], b_ref[...])
        o_refs[0][...] = y
        if len(o_refs) > 1:
            o_refs[1][...] = y.astype(BF16)

    @pl.when(i == 0)
    def _():
        matmul()

    @pl.when((i > 0) & (i < n))
    def _():
        norm()
        matmul()

    @pl.when(i == n)
    def _():
        norm()


def _proj_resid_ln(u, w, resid, g, b, tm, with_bf16):
    t, k = u.shape
    n = t // tm
    row_in = lambda i: (jnp.minimum(i, n - 1), 0)
    row_out = lambda i: (jnp.maximum(i - 1, 0), 0)
    fixed = lambda i: (0, 0)
    out_shape = [jax.ShapeDtypeStruct((t, D_MODEL), F32)]
    if with_bf16:
        out_shape.append(jax.ShapeDtypeStruct((t, D_MODEL), BF16))
    return pl.pallas_call(
        functools.partial(_resid_ln_kernel, n=n),
        grid=(n + 1,),
        in_specs=[pl.BlockSpec((tm, k), row_in),
                  _resident((k, D_MODEL), fixed),
                  pl.BlockSpec((tm, D_MODEL), row_in),
                  _resident((1, D_MODEL), fixed),
                  _resident((1, D_MODEL), fixed)],
        out_specs=[pl.BlockSpec((tm, D_MODEL), row_out) for _ in out_shape],
        out_shape=out_shape,
        scratch_shapes=[pltpu.VMEM((tm, D_MODEL), F32)],
        compiler_params=_params(1),
        name="proj_resid_ln",
    )(u, w, resid, g, b)


def _swiglu_kernel(h_ref, wg_ref, wu_ref, o_ref, wg_bf, wu_bf):
    @pl.when(pl.program_id(1) == 0)
    def _():
        wg_bf[...] = wg_ref[...].astype(BF16)
        wu_bf[...] = wu_ref[...].astype(BF16)

    h = h_ref[...]
    gate = _dot(h, wg_bf[...])
    up = _dot(h, wu_bf[...])
    o_ref[...] = (gate * jax.nn.sigmoid(gate) * up).astype(BF16)


def _swiglu_in(hb, w_ffn_in, tm=TM_SWIGLU, tn=TN_SWIGLU):
    t = hb.shape[0]
    nj = D_FF // tn
    return pl.pallas_call(
        _swiglu_kernel,
        grid=(nj, t // tm),
        in_specs=[pl.BlockSpec((tm, D_MODEL), lambda j, i: (i, 0)),
                  pl.BlockSpec((D_MODEL, tn), lambda j, i: (0, j)),
                  pl.BlockSpec((D_MODEL, tn), lambda j, i: (0, nj + j))],
        out_specs=pl.BlockSpec((tm, tn), lambda j, i: (i, j)),
        out_shape=jax.ShapeDtypeStruct((t, D_FF), BF16),
        scratch_shapes=[pltpu.VMEM((D_MODEL, tn), BF16), pltpu.VMEM((D_MODEL, tn), BF16)],
        compiler_params=_params(2),
        name="swiglu_in",
    )(hb, w_ffn_in, w_ffn_in)


def _rope_tables(seq):
    inv = 1.0 / (ROPE_THETA ** (jnp.arange(ROPE_HALF, dtype=jnp.float32) / ROPE_HALF))
    ang = jnp.arange(seq, dtype=jnp.float32)[:, None] * inv[None, :]
    cos, sin = jnp.cos(ang), jnp.sin(ang)
    z = jnp.zeros_like(cos)
    return (jnp.concatenate([cos, cos, z, z], axis=-1),
            jnp.concatenate([-sin, z, z, z], axis=-1),
            jnp.concatenate([z, sin, z, z], axis=-1))


def _prepare_weights(w_in, w_uq, w_ukv):
    parts = np.cumsum((A_HEADS * HEAD_DIM, A_KV_HEADS * HEAD_DIM, A_KV_HEADS * HEAD_DIM, Q_LORA, KV_LORA, QK_ROPE))
    qa, ka, va, cq, ckv, kr, qc = jnp.split(w_in, [int(p) for p in parts], axis=-1)
    pad = jnp.zeros((D_MODEL, LANES - QK_ROPE), w_in.dtype)
    w_in_p = jnp.concatenate([p.astype(BF16) for p in (qa, ka, va, cq, ckv, qc, kr, pad)], axis=-1)
    w_uq_p = jnp.pad(w_uq.reshape(Q_LORA, B_HEADS, QK_NOPE + QK_ROPE),
                     ((0, 0), (0, 0), (0, QK_PAD - QK_NOPE - QK_ROPE))).reshape(Q_LORA, B_HEADS * QK_PAD).astype(BF16)
    kv = w_ukv.reshape(KV_LORA, B_HEADS, 2, QK_NOPE)
    w_ukv_p = jnp.concatenate([kv[:, :, 0].reshape(KV_LORA, -1), kv[:, :, 1].reshape(KV_LORA, -1)],
                              axis=-1).astype(BF16)
    return w_in_p, w_uq_p, w_ukv_p


def _encoder_layer(x, mem, bias, w):
    b, seq, d = x.shape
    t = b * seq
    x2 = x.reshape(t, d)
    rope_c, rope_sa, rope_sb = _rope_tables(seq)
    xb, qa, ka, va, qc, qm, km, vt = _project(
        x2, w["w_in"], w["q_norm_g"], w["w_uq"], w["kv_norm_g"], w["w_ukv"], rope_c, rope_sa, rope_sb, seq)
    r3 = lambda v: v.reshape(b, seq, v.shape[-1])
    mkv = _memory_kv(mem.reshape(b * N_MEM, d), w["w_mem_kv"]).reshape(b, N_MEM, -1)
    a_out, c_out = _window_and_cross_attention(r3(qa), r3(ka), r3(va), bias, w["sink"], r3(qc), mkv)
    b_out = _mla_attention(r3(qm), r3(km), vt)
    merged = _gated_merge(xb, a_out.reshape(t, -1), b_out.reshape(t, -1), c_out.reshape(t, -1),
                          w["w_gate"], w["b_gate"], w["w_br_a"], w["w_br_b"], w["w_br_c"])
    h, hb = _proj_resid_ln(merged, w["w_o"], x2, w["ln1_g"], w["ln1_b"], tm=TM_OUT_PROJ, with_bf16=True)
    act = _swiglu_in(hb, w["w_ffn_in"])
    (y,) = _proj_resid_ln(act, w["w_ffn_down"], h, w["ln2_g"], w["ln2_b"], tm=TM_FFN_DOWN, with_bf16=False)
    return y.reshape(b, seq, d)


def kernel(x_prompt, x_sample, mem_prompt, mem_sample, w_in, rel_bias, sink, q_norm_g, w_uq, kv_norm_g, w_ukv,
           w_mem_kv, w_gate, b_gate, w_br_a, w_br_b, w_br_c, w_o, ln1_g, ln1_b, w_ffn_in, w_ffn_down, ln2_g, ln2_b):
    bias = _window_bias(rel_bias)
    xp, xs = x_prompt, x_sample
    for l in range(DEPTH):
        w_in_p, w_uq_p, w_ukv_p = _prepare_weights(w_in[l], w_uq[l], w_ukv[l])
        w = dict(
            w_in=w_in_p, w_uq=w_uq_p, w_ukv=w_ukv_p, sink=sink[l],
            q_norm_g=q_norm_g[l][None, :], kv_norm_g=kv_norm_g[l][None, :],
            w_mem_kv=w_mem_kv[l].astype(BF16), w_gate=w_gate[l].astype(BF16), b_gate=b_gate[l][None, :],
            w_br_a=w_br_a[l].astype(BF16), w_br_b=w_br_b[l].astype(BF16), w_br_c=w_br_c[l].astype(BF16),
            w_o=w_o[l].astype(BF16), ln1_g=ln1_g[l][None, :], ln1_b=ln1_b[l][None, :],
            w_ffn_in=w_ffn_in[l], w_ffn_down=w_ffn_down[l].astype(BF16),
            ln2_g=ln2_g[l][None, :], ln2_b=ln2_b[l][None, :])
        xp = _encoder_layer(xp, mem_prompt, bias, w)
        xs = _encoder_layer(xs, mem_sample, bias, w)
    return (xp, xs)
```
